```python
import math
import jax, jax.numpy as jnp
from jax import lax
import numpy as np


D_MODEL = 1024
BATCH = 4
SEQ = 8192
DEPTH = 2

HEAD_DIM = 64
SB_HEADS = D_MODEL // 128
DIFF_HEADS = D_MODEL // 256
DIFF_VDIM = 2 * HEAD_DIM
SB_WIDTH = SB_HEADS * HEAD_DIM
DIFF_QK_WIDTH = DIFF_HEADS * 2 * HEAD_DIM
DIFF_V_WIDTH = DIFF_HEADS * DIFF_VDIM
IN_WIDTH = 3 * SB_WIDTH + 2 * DIFF_QK_WIDTH + DIFF_V_WIDTH
N_BRANCHES = 2
D_FF = 11 * D_MODEL // 4
N_EXPERTS = 8
TOP_K = 2
D_FF_EXPERT = 7 * D_MODEL // 2
BLOCK_Q = 128
MOE_BLOCK = 512
NORM_EPS = 1e-6
N_DENSE = (DEPTH + 1) // 2
N_MOE = DEPTH // 2

kernel_name = 'hybrid_stickbreak_diffattn_moe_block'


def rms_norm(x, g):
    xf = x.astype(jnp.float32)
    y = xf * lax.rsqrt(jnp.mean(xf * xf, axis=-1, keepdims=True) + NORM_EPS)
    return (y * g.astype(jnp.float32)).astype(x.dtype)


def alibi_slopes(n_heads):
    return jnp.asarray([2.0 ** (-8.0 * (h + 1) / n_heads) for h in range(n_heads)], dtype=jnp.float32)


def stick_breaking_attention(q, k, v):
    B, S, H, d = q.shape
    nb = S // BLOCK_Q
    scale = d ** -0.5
    qb = q.reshape(B, nb, BLOCK_Q, H, d).transpose(1, 0, 3, 2, 4)
    k_pos = jnp.arange(S)

    def one_block(args):
        q_blk, blk = args
        z = jnp.einsum('bhqd,bkhd->bhqk', q_blk, k).astype(jnp.float32) * scale
        q_pos = blk * BLOCK_Q + jnp.arange(BLOCK_Q)
        causal = k_pos[None, :] < q_pos[:, None]
        log_beta = jax.nn.log_sigmoid(z)
        log_keep = jnp.where(causal, jax.nn.log_sigmoid(-z), 0.0)
        log_survive = lax.cumsum(log_keep, axis=3, reverse=True) - log_keep
        w = jnp.where(causal, jnp.exp(log_beta + log_survive), 0.0)
        return jnp.einsum('bhqk,bkhd->bqhd', w.astype(v.dtype), v)

    o = lax.map(one_block, (qb, jnp.arange(nb)))
    return o.transpose(1, 0, 2, 3, 4).reshape(B, S, H * d)


def differential_attention(q, k, v, lam, lambda_init, subln_g):
    B, S, H, _, d = q.shape
    nb = S // BLOCK_Q
    scale = d ** -0.5
    slopes = alibi_slopes(H)
    qb = q.reshape(B, nb, BLOCK_Q, H, 2, d).transpose(1, 0, 3, 4, 2, 5)
    k_pos = jnp.arange(S)

    def one_block(args):
        q_blk, blk = args
        s = jnp.einsum('bhcqd,bkhcd->bhcqk', q_blk, k).astype(jnp.float32) * scale
        q_pos = blk * BLOCK_Q + jnp.arange(BLOCK_Q)
        dist = q_pos[:, None] - k_pos[None, :]
        bias = -slopes[:, None, None] * dist.astype(jnp.float32)
        s = jnp.where(dist >= 0, s + bias[None, :, None], -jnp.inf)
        p = jax.nn.softmax(s, axis=-1)
        attn = p[:, :, 0] - lam * p[:, :, 1]
        return jnp.einsum('bhqk,bkhe->bqhe', attn.astype(v.dtype), v)

    o = lax.map(one_block, (qb, jnp.arange(nb)))
    o = o.transpose(1, 0, 2, 3, 4).reshape(B, S, H, 2 * d)
    o = rms_norm(o, subln_g) * (1.0 - lambda_init)
    return o.reshape(B, S, H * 2 * d)


def swiglu(h, w_gate_up, w_down):
    g, u = jnp.split(h @ w_gate_up, 2, axis=-1)
    return (jax.nn.silu(g) * u) @ w_down


def moe_swiglu(h, w_router, w1, w3, w2):
    B, S, D = h.shape
    n_tok = B * S
    n_assign = n_tok * TOP_K
    n_slots = -(-n_assign // MOE_BLOCK) * MOE_BLOCK + N_EXPERTS * MOE_BLOCK
    n_blocks = n_slots // MOE_BLOCK
    t = h.reshape(n_tok, D)
    logits = (t @ w_router).astype(jnp.float32)
    top_logit, top_idx = lax.top_k(logits, TOP_K)
    top_w = jax.nn.softmax(top_logit, axis=-1)
    flat_e = top_idx.reshape(-1)
    flat_tok = jnp.repeat(jnp.arange(n_tok, dtype=jnp.int32), TOP_K)
    flat_w = top_w.reshape(-1)
    order = jnp.argsort(flat_e)
    e_sorted = flat_e[order]
    counts = jnp.bincount(flat_e, length=N_EXPERTS)
    padded = (counts + MOE_BLOCK - 1) // MOE_BLOCK * MOE_BLOCK
    start = jnp.cumsum(counts) - counts
    padded_end = jnp.cumsum(padded)
    padded_start = padded_end - padded
    slot = padded_start[e_sorted] + jnp.arange(n_assign) - start[e_sorted]
    slot_tok = jnp.zeros((n_slots,), jnp.int32).at[slot].set(flat_tok[order])
    slot_w = jnp.zeros((n_slots,), jnp.float32).at[slot].set(flat_w[order])
    block_e = jnp.minimum(
        jnp.searchsorted(padded_end, jnp.arange(n_blocks) * MOE_BLOCK, side='right'),
        N_EXPERTS - 1)
    xs = t[slot_tok].reshape(n_blocks, MOE_BLOCK, D)

    def expert_block(args):
        xb, e = args
        hid = jax.nn.silu(xb @ w1[e]) * (xb @ w3[e])
        return hid @ w2[e]

    ys = lax.map(expert_block, (xs, block_e)).reshape(n_slots, D)
    ys = ys * slot_w[:, None].astype(ys.dtype)
    out = jax.ops.segment_sum(ys, slot_tok, num_segments=n_tok)
    return out.reshape(B, S, D)


def _normal(k, shape, scale):
    return jax.random.normal(k, shape, jnp.float32) * scale


def setup_inputs(seed: int = 0) -> dict:
    key = jax.random.key(seed)
    ks = jax.random.split(key, 20)
    D = D_MODEL
    return {
        'x': _normal(ks[0], (BATCH, SEQ, D), 1.0),
        'norm_mix_g': 1.0 + _normal(ks[1], (DEPTH, D), 0.01),
        'w_in': _normal(ks[2], (DEPTH, D, IN_WIDTH), D ** -0.5),
        'diff_q_norm_g': 1.0 + _normal(ks[3], (DEPTH, HEAD_DIM), 0.01),
        'diff_k_norm_g': 1.0 + _normal(ks[4], (DEPTH, HEAD_DIM), 0.01),
        'diff_lambda': _normal(ks[5], (DEPTH, 4, HEAD_DIM), 0.1),
        'diff_subln_g': 1.0 + _normal(ks[6], (DEPTH, DIFF_VDIM), 0.01),
        'w_gate': _normal(ks[7], (DEPTH, D, N_BRANCHES * D), D ** -0.5),
        'b_gate': _normal(ks[8], (DEPTH, N_BRANCHES * D), 0.02),
        'w_branch_sb': _normal(ks[9], (DEPTH, SB_WIDTH, D), SB_WIDTH ** -0.5),
        'w_branch_diff': _normal(ks[10], (DEPTH, DIFF_V_WIDTH, D), DIFF_V_WIDTH ** -0.5),
        'w_out': _normal(ks[11], (DEPTH, D, D), D ** -0.5),
        'norm_ffn_g': 1.0 + _normal(ks[12], (DEPTH, D), 0.01),
        'ffn_w_gate_up': _normal(ks[13], (N_DENSE, D, 2 * D_FF), D ** -0.5),
        'ffn_w_down': _normal(ks[14], (N_DENSE, D_FF, D), D_FF ** -0.5),
        'moe_w_router': _normal(ks[15], (N_MOE, D, N_EXPERTS), D ** -0.5),
        'moe_w1': _normal(ks[16], (N_MOE, N_EXPERTS, D, D_FF_EXPERT), D ** -0.5),
        'moe_w3': _normal(ks[17], (N_MOE, N_EXPERTS, D, D_FF_EXPERT), D ** -0.5),
        'moe_w2': _normal(ks[18], (N_MOE, N_EXPERTS, D_FF_EXPERT, D), D_FF_EXPERT ** -0.5),
    }


def reference(x, norm_mix_g, w_in, diff_q_norm_g, diff_k_norm_g, diff_lambda, diff_subln_g,
              w_gate, b_gate, w_branch_sb, w_branch_diff, w_out, norm_ffn_g,
              ffn_w_gate_up, ffn_w_down, moe_w_router, moe_w1, moe_w3, moe_w2):
    B, S, D = x.shape
    split_at = [int(c) for c in np.cumsum([SB_WIDTH, SB_WIDTH, SB_WIDTH, DIFF_QK_WIDTH, DIFF_QK_WIDTH])]
    for i in range(DEPTH):
        h = rms_norm(x, norm_mix_g[i])
        proj = h @ w_in[i]
        sb_q, sb_k, sb_v, df_q, df_k, df_v = jnp.split(proj, split_at, axis=-1)
        y_sb = stick_breaking_attention(
            sb_q.reshape(B, S, SB_HEADS, HEAD_DIM),
            sb_k.reshape(B, S, SB_HEADS, HEAD_DIM),
            sb_v.reshape(B, S, SB_HEADS, HEAD_DIM))
        df_q = rms_norm(df_q.reshape(B, S, DIFF_HEADS, 2, HEAD_DIM), diff_q_norm_g[i])
        df_k = rms_norm(df_k.reshape(B, S, DIFF_HEADS, 2, HEAD_DIM), diff_k_norm_g[i])
        lambda_init = 0.8 - 0.6 * math.exp(-0.3 * i)
        lq1, lk1, lq2, lk2 = diff_lambda[i].astype(jnp.float32)
        lam = jnp.exp(jnp.sum(lq1 * lk1)) - jnp.exp(jnp.sum(lq2 * lk2)) + lambda_init
        y_df = differential_attention(
            df_q, df_k, df_v.reshape(B, S, DIFF_HEADS, DIFF_VDIM),
            lam, lambda_init, diff_subln_g[i])
        gates = jax.nn.sigmoid(h @ w_gate[i] + b_gate[i]).reshape(B, S, N_BRANCHES, D)
        merged = gates[:, :, 0] * (y_sb @ w_branch_sb[i]) + gates[:, :, 1] * (y_df @ w_branch_diff[i])
        x = x + merged @ w_out[i]
        h = rms_norm(x, norm_ffn_g[i])
        j = i // 2
        if i % 2 == 0:
            x = x + swiglu(h, ffn_w_gate_up[j], ffn_w_down[j])
        else:
            x = x + moe_swiglu(h, moe_w_router[j], moe_w1[j], moe_w3[j], moe_w2[j])
    return x
```

```python
import functools
import math

import jax
import jax.numpy as jnp
from jax import lax
from jax.experimental import pallas as pl
from jax.experimental.pallas import tpu as pltpu

F32 = jnp.float32
BF16 = jnp.bfloat16

HEAD_DIM = 64
NORM_EPS = 1e-6
N_EXPERTS = 8
TOP_K = 2
MOE_BLOCK = 512
LANES = 128
VMEM_LIMIT = 56 * 1024 * 1024

EXP_UNDERFLOW = -104.0


def _dot(a, b):
    return jnp.dot(a, b, preferred_element_type=F32)


def _dot_nt(a, b):
    return lax.dot_general(a, b, (((1,), (1,)), ((), ())), preferred_element_type=F32)


def _rms(x, g):
    ms = jnp.mean(x * x, axis=-1, keepdims=True)
    return (x * lax.rsqrt(ms + NORM_EPS)) * g


def _dot_split(x, m):
    hi = x.astype(BF16)
    lo = (x - hi.astype(F32)).astype(BF16)
    return _dot(hi, m) + _dot(lo, m)


def _params(sem, vmem=VMEM_LIMIT):
    return pltpu.CompilerParams(dimension_semantics=sem, vmem_limit_bytes=vmem)


def _proj_kernel(x_ref, g_ref, w_ref, bd_ref, gq_ref, gk_ref,
                 sbq_ref, sbk_ref, sbv_ref, dfq_ref, dfk_ref, dfv_ref, *, width):
    scale = HEAD_DIM ** -0.5
    h = _rms(x_ref[...], g_ref[...]).astype(BF16)

    def group_norm(p, g):
        ss = _dot_split(p * p, bd_ref[...])
        return (p * lax.rsqrt(ss * (1.0 / HEAD_DIM) + NORM_EPS)) * g

    outs = (sbq_ref, sbk_ref, sbv_ref, dfq_ref, dfk_ref, dfv_ref)
    for j, o_ref in enumerate(outs):
        p = _dot(h, w_ref[:, j * width:(j + 1) * width])
        if j == 0:
            p = p * scale
        elif j == 3:
            p = group_norm(p, gq_ref[...]) * scale
        elif j == 4:
            p = group_norm(p, gk_ref[...])
        o_ref[...] = p.astype(BF16)


def _proj_call(x, g, w_in, gq, gk, *, tm):
    n, d = x.shape
    width = w_in.shape[1] // 6
    grp = lax.broadcasted_iota(jnp.int32, (width, width), 0) // HEAD_DIM
    bd = (grp == grp.T).astype(BF16)
    row = pl.BlockSpec((tm, d), lambda i: (i, 0))
    const = lambda shape: pl.BlockSpec(shape, lambda i: (0, 0))
    out = pl.BlockSpec((tm, width), lambda i: (i, 0))
    return pl.pallas_call(
        functools.partial(_proj_kernel, width=width),
        grid=(n // tm,),
        in_specs=[row, const((1, d)), const(w_in.shape), const((width, width)),
                  const((1, width)), const((1, width))],
        out_specs=[out] * 6,
        out_shape=[jax.ShapeDtypeStruct((n, width), BF16)] * 6,
        compiler_params=_params(("parallel",)),
        name="proj",
    )(x, g.reshape(1, d), w_in, bd,
      jnp.tile(gq, width // HEAD_DIM).reshape(1, width),
      jnp.tile(gk, width // HEAD_DIM).reshape(1, width))


def _sb_kernel(q_ref, k_ref, v_ref, o_ref, acc_ref, l_ref, *, tq):
    qi = pl.program_id(2)
    row = lax.broadcasted_iota(jnp.int32, (tq, tq), 0)
    col = lax.broadcasted_iota(jnp.int32, (tq, tq), 1)
    suffix = (row > col).astype(BF16)
    causal = col < row

    for hh in range(LANES // HEAD_DIM):
        lanes = slice(hh * HEAD_DIM, (hh + 1) * HEAD_DIM)
        q = q_ref[:, lanes]

        def block(kb, carry_l, masked, q=q, lanes=lanes):
            start = pl.multiple_of(kb * tq, tq)
            k = k_ref[pl.ds(start, tq), lanes]
            v = v_ref[pl.ds(start, tq), lanes]
            z = _dot_nt(q, k)
            sp = jnp.maximum(z, 0.0) + jnp.log1p(jnp.exp(-jnp.abs(z)))
            log_keep = -sp
            if masked:
                log_keep = jnp.where(causal, log_keep, 0.0)
            later = _dot_split(log_keep, suffix)
            w = jnp.exp((z - sp) + later + carry_l)
            if masked:
                w = jnp.where(causal, w, 0.0)
            pv = _dot(w.astype(BF16), v)
            return pv, carry_l + later[:, :1] + log_keep[:, :1]

        pv0, l0 = block(qi, jnp.zeros((tq, 1), F32), True)
        acc_ref[hh] = pv0
        l_ref[hh] = l0

        def cond(c):
            it, lmax = c
            return jnp.logical_and(it < qi, lmax > EXP_UNDERFLOW)

        def body(c, hh=hh, block=block):
            it, _ = c
            pv, l_new = block(qi - 1 - it, l_ref[hh], False)
            acc_ref[hh] += pv
            l_ref[hh] = l_new
            return it + 1, jnp.max(l_new)

        lax.while_loop(cond, body, (jnp.int32(0), jnp.max(l0)))

    o_ref[...] = jnp.concatenate(
        [acc_ref[hh] for hh in range(LANES // HEAD_DIM)], axis=-1).astype(o_ref.dtype)


def _sb_call(q, k, v, *, batch, tq):
    n, width = q.shape
    seq = n // batch
    nq = seq // tq
    qspec = pl.BlockSpec((tq, LANES), lambda b, h, i: (b * nq + i, h))
    kvspec = pl.BlockSpec((seq, LANES), lambda b, h, i: (b, h))
    return pl.pallas_call(
        functools.partial(_sb_kernel, tq=tq),
        grid=(batch, width // LANES, nq),
        in_specs=[qspec, kvspec, kvspec],
        out_specs=qspec,
        out_shape=jax.ShapeDtypeStruct((n, width), BF16),
        scratch_shapes=[pltpu.VMEM((LANES // HEAD_DIM, tq, HEAD_DIM), F32),
                        pltpu.VMEM((LANES // HEAD_DIM, tq, 1), F32)],
        compiler_params=_params(("parallel", "parallel", "arbitrary")),
        name="sb_attn",
    )(q, k, v)


def _diff_kernel(slopes_ref, lam_ref, g_ref, q_ref, k_ref, v_ref, o_ref, *, tq, lambda_init):
    h = pl.program_id(1)
    qi = pl.program_id(2)
    slope = slopes_ref[h]
    row = lax.broadcasted_iota(jnp.int32, (tq, tq), 0)
    col = lax.broadcasted_iota(jnp.int32, (tq, tq), 1)
    rel_bias = -slope * (row - col).astype(F32)
    visible = col <= row
    qs = [q_ref[:, c * HEAD_DIM:(c + 1) * HEAD_DIM] for c in range(2)]

    def block(kb, carry, masked):
        start = pl.multiple_of(kb * tq, tq)
        kblk = k_ref[pl.ds(start, tq), :]
        v = v_ref[pl.ds(start, tq), :]
        bias = rel_bias - slope * ((qi - kb) * tq).astype(F32)
        new = []
        for c in range(2):
            m, l, acc = carry[c]
            s = _dot_nt(qs[c], kblk[:, c * HEAD_DIM:(c + 1) * HEAD_DIM]) + bias
            if masked:
                s = jnp.where(visible, s, -1e30)
            m_new = jnp.maximum(m, jnp.max(s, axis=-1, keepdims=True))
            alpha = jnp.exp(m - m_new)
            p = jnp.exp(s - m_new)
            l = alpha * l + jnp.sum(p, axis=-1, keepdims=True)
            acc = alpha * acc + _dot(p.astype(BF16), v)
            new.append((m_new, l, acc))
        return tuple(new)

    init = tuple((jnp.full((tq, 1), -1e30, F32), jnp.zeros((tq, 1), F32),
                  jnp.zeros((tq, LANES), F32)) for _ in range(2))
    carry = lax.fori_loop(0, qi, lambda kb, c: block(kb, c, False), init)
    (_, l0, a0), (_, l1, a1) = block(qi, carry, True)

    lp = lam_ref[...]
    lam = (jnp.exp(jnp.sum(lp[0:1] * lp[1:2], axis=-1, keepdims=True))
           - jnp.exp(jnp.sum(lp[2:3] * lp[3:4], axis=-1, keepdims=True)) + lambda_init)
    o = a0 / l0 - lam * (a1 / l1)
    o_ref[...] = (_rms(o, g_ref[...]) * (1.0 - lambda_init)).astype(o_ref.dtype)


def _diff_call(q, k, v, lam_params, subln_g, *, batch, tq, lambda_init):
    n, width = q.shape
    heads = width // LANES
    seq = n // batch
    nq = seq // tq
    slopes = jnp.asarray([2.0 ** (-8.0 * (h + 1) / heads) for h in range(heads)], dtype=F32)
    qspec = pl.BlockSpec((tq, LANES), lambda b, h, i: (b * nq + i, h))
    kvspec = pl.BlockSpec((seq, LANES), lambda b, h, i: (b, h))
    return pl.pallas_call(
        functools.partial(_diff_kernel, tq=tq, lambda_init=lambda_init),
        grid=(batch, heads, nq),
        in_specs=[pl.BlockSpec(memory_space=pltpu.SMEM),
                  pl.BlockSpec(lam_params.shape, lambda b, h, i: (0, 0)),
                  pl.BlockSpec((1, LANES), lambda b, h, i: (0, 0)),
                  qspec, kvspec, kvspec],
        out_specs=qspec,
        out_shape=jax.ShapeDtypeStruct((n, width), BF16),
        compiler_params=_params(("parallel", "parallel", "arbitrary")),
        name="diff_attn",
    )(slopes, lam_params.astype(F32), subln_g.reshape(1, LANES), q, k, v)


def _merge_kernel(*refs, router):
    (x_ref, ysb_ref, ydf_ref, gmix_ref, wg_ref, bg_ref, wsb_ref, wdf_ref, wo_ref,
     gffn_ref) = refs[:10]
    rest = refs[10:]
    x = x_ref[...]
    d = x.shape[-1]
    h = _rms(x, gmix_ref[...]).astype(BF16)
    gates = jax.nn.sigmoid(_dot(h, wg_ref[...]) + bg_ref[...])
    merged = (gates[:, :d] * _dot(ysb_ref[...], wsb_ref[...])
              + gates[:, d:] * _dot(ydf_ref[...], wdf_ref[...]))
    xn = x + _dot(merged.astype(BF16), wo_ref[...])
    h2 = _rms(xn, gffn_ref[...])
    if router:
        wr_ref, xo_ref, h2_ref, lg_ref = rest
        lg_ref[...] = _dot(h2.astype(BF16), wr_ref[...])
    else:
        xo_ref, h2_ref = rest
    xo_ref[...] = xn
    h2_ref[...] = h2.astype(h2_ref.dtype)


def _merge_call(x, ysb, ydf, gmix, wg, bg, wsb, wdf, wo, gffn, w_router, *, tm):
    n, d = x.shape
    router = w_router is not None
    row = lambda w: pl.BlockSpec((tm, w), lambda i: (i, 0))
    const = lambda a: pl.BlockSpec(a.shape, lambda i: (0, 0))
    args = [x, ysb, ydf, gmix.reshape(1, d), wg, bg.reshape(1, -1), wsb, wdf, wo,
            gffn.reshape(1, d)]
    in_specs = [row(d), row(ysb.shape[1]), row(ydf.shape[1])] + [const(a) for a in args[3:]]
    out_specs = [row(d), row(d)]
    out_shape = [jax.ShapeDtypeStruct((n, d), F32),
                 jax.ShapeDtypeStruct((n, d), F32 if router else BF16)]
    if router:
        wr = jnp.zeros((d, LANES), BF16).at[:, :w_router.shape[1]].set(w_router.astype(BF16))
        args.append(wr)
        in_specs.append(const(wr))
        out_specs.append(row(LANES))
        out_shape.append(jax.ShapeDtypeStruct((n, LANES), F32))
    return pl.pallas_call(
        functools.partial(_merge_kernel, router=router),
        grid=(n // tm,),
        in_specs=in_specs,
        out_specs=out_specs,
        out_shape=out_shape,
        compiler_params=_params(("parallel",)),
        name="merge_router" if router else "merge",
    )(*args)


def _swiglu_kernel(x_ref, h_ref, wg_ref, wu_ref, wd_ref, o_ref, acc_ref):
    f = pl.program_id(1)

    @pl.when(f == 0)
    def _():
        acc_ref[...] = jnp.zeros_like(acc_ref)

    h = h_ref[...]
    g = _dot(h, wg_ref[...])
    u = _dot(h, wu_ref[...])
    act = (g * jax.nn.sigmoid(g)) * u
    acc_ref[...] += _dot(act.astype(BF16), wd_ref[...])

    @pl.when(f == pl.num_programs(1) - 1)
    def _():
        o_ref[...] = x_ref[...] + acc_ref[...]


def _swiglu_call(x, h, w_gate_up, w_down, *, tm, nf):
    n, d = x.shape
    d_ff = w_down.shape[0]
    tf = d_ff // nf
    assert tf * nf == d_ff and tf % LANES == 0
    row = pl.BlockSpec((tm, d), lambda i, f: (i, 0))
    return pl.pallas_call(
        _swiglu_kernel,
        grid=(n // tm, nf),
        in_specs=[row, row,
                  pl.BlockSpec((d, tf), lambda i, f: (0, f)),
                  pl.BlockSpec((d, tf), lambda i, f: (0, f + nf)),
                  pl.BlockSpec((tf, d), lambda i, f: (f, 0))],
        out_specs=row,
        out_shape=jax.ShapeDtypeStruct((n, d), F32),
        scratch_shapes=[pltpu.VMEM((tm, d), F32)],
        compiler_params=_params(("parallel", "arbitrary")),
        name="swiglu",
    )(x, h, w_gate_up, w_gate_up, w_down)


def _row_copy(src_hbm, src_row, dst_buf, dst_row, sem):
    return pltpu.make_async_copy(src_hbm.at[pl.ds(src_row, 1)], dst_buf.at[pl.ds(dst_row, 1)], sem)


def _expert_kernel(be_ref, tok_ref, nused_ref, h_hbm, sw_ref, w1_ref, w3_ref, w2_ref, o_ref,
                   xbuf, xb_ref, acc_ref, sem):
    del be_ref
    i = pl.program_id(0)
    f = pl.program_id(1)
    nf = pl.num_programs(1)
    nused = nused_ref[0]
    rows = xb_ref.shape[0]

    def gather(blk, slot, wait):
        def body(r, carry):
            cp = _row_copy(h_hbm, tok_ref[blk * rows + r], xbuf.at[slot], r, sem.at[slot])
            if wait:
                cp.wait()
            else:
                cp.start()
            return carry
        lax.fori_loop(0, rows, body, 0)

    @pl.when(jnp.logical_and(f == 0, i < nused))
    def _():
        @pl.when(i == 0)
        def _():
            gather(0, 0, wait=False)

        @pl.when(i + 1 < nused)
        def _():
            gather(i + 1, (i + 1) % 2, wait=False)

        gather(i, i % 2, wait=True)
        xb_ref[...] = xbuf[i % 2].astype(BF16)
        acc_ref[...] = jnp.zeros_like(acc_ref)

    @pl.when(i < nused)
    def _():
        xb = xb_ref[...]
        g = _dot(xb, w1_ref[...])
        u = _dot(xb, w3_ref[...])
        hid = (g * jax.nn.sigmoid(g)) * u
        acc_ref[...] += _dot(hid.astype(BF16), w2_ref[...])

    @pl.when(f == nf - 1)
    def _():
        @pl.when(i < nused)
        def _():
            o_ref[...] = acc_ref[...] * sw_ref[...]

        @pl.when(i >= nused)
        def _():
            o_ref[...] = jnp.zeros_like(o_ref)


def _expert_call(block_e, slot_tok, nused, h, slot_w, w1, w3, w2, *, nf):
    n_slots = slot_tok.shape[0]
    n_blocks = n_slots // MOE_BLOCK
    d = h.shape[1]
    d_ff = w1.shape[2]
    tf = d_ff // nf
    assert tf * nf == d_ff and tf % LANES == 0
    grid_spec = pltpu.PrefetchScalarGridSpec(
        num_scalar_prefetch=3,
        grid=(n_blocks, nf),
        in_specs=[pl.BlockSpec(memory_space=pl.ANY),
                  pl.BlockSpec((MOE_BLOCK, 1), lambda i, f, be, tok, nu: (i, 0)),
                  pl.BlockSpec((None, d, tf), lambda i, f, be, tok, nu: (be[i], 0, f)),
                  pl.BlockSpec((None, d, tf), lambda i, f, be, tok, nu: (be[i], 0, f)),
                  pl.BlockSpec((None, tf, d), lambda i, f, be, tok, nu: (be[i], f, 0))],
        out_specs=pl.BlockSpec((MOE_BLOCK, d), lambda i, f, be, tok, nu: (i, 0)),
        scratch_shapes=[pltpu.VMEM((2, MOE_BLOCK, d), F32),
                        pltpu.VMEM((MOE_BLOCK, d), BF16),
                        pltpu.VMEM((MOE_BLOCK, d), F32),
                        pltpu.SemaphoreType.DMA((2,))],
    )
    return pl.pallas_call(
        _expert_kernel,
        grid_spec=grid_spec,
        out_shape=jax.ShapeDtypeStruct((n_slots, d), F32),
        compiler_params=_params(("arbitrary", "arbitrary")),
        name="moe_experts",
    )(block_e, slot_tok, nused, h, slot_w, w1, w3, w2)


def _combine_kernel(slot_ref, x_ref, ys_hbm, o_ref, buf, sem):
    i = pl.program_id(0)
    n_steps = pl.num_programs(0)
    tm = x_ref.shape[0]

    def gather(step, slot, wait):
        def body(r, carry):
            for k in range(TOP_K):
                cp = _row_copy(ys_hbm, slot_ref[(step * tm + r) * TOP_K + k],
                               buf.at[slot, k], r, sem.at[slot])
                if wait:
                    cp.wait()
                else:
                    cp.start()
            return carry
        lax.fori_loop(0, tm, body, 0)

    @pl.when(i == 0)
    def _():
        gather(0, 0, wait=False)

    @pl.when(i + 1 < n_steps)
    def _():
        gather(i + 1, (i + 1) % 2, wait=False)

    gather(i, i % 2, wait=True)
    y = buf[i % 2]
    o_ref[...] = x_ref[...] + (y[0] + y[1])


def _combine_call(slots, x, ys, *, tm):
    n, d = x.shape
    grid_spec = pltpu.PrefetchScalarGridSpec(
        num_scalar_prefetch=1,
        grid=(n // tm,),
        in_specs=[pl.BlockSpec((tm, d), lambda i, s: (i, 0)),
                  pl.BlockSpec(memory_space=pl.ANY)],
        out_specs=pl.BlockSpec((tm, d), lambda i, s: (i, 0)),
        scratch_shapes=[pltpu.VMEM((2, TOP_K, tm, d), F32),
                        pltpu.SemaphoreType.DMA((2,))],
    )
    return pl.pallas_call(
        _combine_kernel,
        grid_spec=grid_spec,
        out_shape=jax.ShapeDtypeStruct((n, d), F32),
        compiler_params=_params(("arbitrary",)),
        name="moe_combine",
    )(slots, x, ys)


def _route(logits):
    n_tok = logits.shape[0]
    n_assign = n_tok * TOP_K
    n_slots = -(-n_assign // MOE_BLOCK) * MOE_BLOCK + N_EXPERTS * MOE_BLOCK
    n_blocks = n_slots // MOE_BLOCK
    top_logit, top_idx = lax.top_k(logits, TOP_K)
    top_w = jax.nn.softmax(top_logit, axis=-1)
    onehot = jnp.sum(jax.nn.one_hot(top_idx, N_EXPERTS, dtype=jnp.int32), axis=1)
    csum = jnp.cumsum(onehot, axis=0)
    counts = csum[-1]
    padded = (counts + MOE_BLOCK - 1) // MOE_BLOCK * MOE_BLOCK
    padded_end = jnp.cumsum(padded)
    padded_start = padded_end - padded
    rank = jnp.take_along_axis(csum - onehot, top_idx, axis=1)
    slots = (padded_start[top_idx] + rank).astype(jnp.int32)
    flat = slots.reshape(-1)
    tok = jnp.repeat(jnp.arange(n_tok, dtype=jnp.int32), TOP_K)
    slot_tok = jnp.zeros((n_slots,), jnp.int32).at[flat].set(tok)
    slot_w = jnp.zeros((n_slots,), F32).at[flat].set(top_w.reshape(-1))
    block_e = jnp.minimum(
        jnp.searchsorted(padded_end, jnp.arange(n_blocks) * MOE_BLOCK, side='right'),
        N_EXPERTS - 1).astype(jnp.int32)
    nused = (padded_end[-1:] // MOE_BLOCK).astype(jnp.int32)
    return block_e, slot_tok, nused, slot_w.reshape(n_slots, 1), flat


def kernel(x, norm_mix_g, w_in, diff_q_norm_g, diff_k_norm_g, diff_lambda, diff_subln_g,
           w_gate, b_gate, w_branch_sb, w_branch_diff, w_out, norm_ffn_g,
           ffn_w_gate_up, ffn_w_down, moe_w_router, moe_w1, moe_w3, moe_w2):
    batch, seq, d = x.shape
    n = batch * seq
    depth = w_in.shape[0]
    tm = min(512, n)
    tq = min(256, seq)
    xf = x.reshape(n, d).astype(F32)
    bf = lambda w: w.astype(BF16)
    for i in range(depth):
        sbq, sbk, sbv, dfq, dfk, dfv = _proj_call(
            xf, norm_mix_g[i], bf(w_in[i]), diff_q_norm_g[i], diff_k_norm_g[i], tm=tm)
        y_sb = _sb_call(sbq, sbk, sbv, batch=batch, tq=tq)
        lambda_init = 0.8 - 0.6 * math.exp(-0.3 * i)
        y_df = _diff_call(dfq, dfk, dfv, diff_lambda[i], diff_subln_g[i],
                          batch=batch, tq=tq, lambda_init=lambda_init)
        j = i // 2
        dense = i % 2 == 0
        outs = _merge_call(xf, y_sb, y_df, norm_mix_g[i], bf(w_gate[i]), b_gate[i],
                           bf(w_branch_sb[i]), bf(w_branch_diff[i]), bf(w_out[i]),
                           norm_ffn_g[i], None if dense else moe_w_router[j], tm=tm)
        if dense:
            xf, h2 = outs
            xf = _swiglu_call(xf, h2, bf(ffn_w_gate_up[j]), bf(ffn_w_down[j]), tm=tm, nf=2)
        else:
            xf, h2, logits = outs
            block_e, slot_tok, nused, slot_w, slots = _route(logits[:, :N_EXPERTS])
            ys = _expert_call(block_e, slot_tok, nused, h2, slot_w,
                              bf(moe_w1[j]), bf(moe_w3[j]), bf(moe_w2[j]), nf=4)
            xf = _combine_call(slots, xf, ys, tm=min(256, n))
    return xf.reshape(batch, seq, d).astype(x.dtype)
```

```python
import functools
import math

import jax
import jax.numpy as jnp
from jax import lax
from jax.experimental import pallas as pl
from jax.experimental.pallas import tpu as pltpu

F32 = jnp.float32
BF16 = jnp.bfloat16

HEAD_DIM = 64
NORM_EPS = 1e-6
N_EXPERTS = 8
TOP_K = 2
MOE_BLOCK = 512
LANES = 128
VMEM_LIMIT = 56 * 1024 * 1024

EXP_UNDERFLOW = -104.0
MASKED = -1e30


def _dot(a, b):
    return jnp.dot(a, b, preferred_element_type=F32)


def _dot_nt(a, b):
    return lax.dot_general(a, b, (((1,), (1,)), ((), ())), preferred_element_type=F32)


def _rms(x, g):
    ms = jnp.mean(x * x, axis=-1, keepdims=True)
    return (x * lax.rsqrt(ms + NORM_EPS)) * g


def _split(x):
    hi = x.astype(BF16)
    return hi, (x - hi.astype(F32)).astype(BF16)


def _params(sem, vmem=VMEM_LIMIT):
    return pltpu.CompilerParams(dimension_semantics=sem, vmem_limit_bytes=vmem)


def _proj_kernel(x_ref, g_ref, wk_ref, wt_ref, bd_ref, gq_ref, gk_ref,
                 sbk_ref, dfk_ref, sbq_ref, sbv_ref, dfq_ref, dfv_ref, *, width):
    scale = HEAD_DIM ** -0.5
    h = _rms(x_ref[...], g_ref[...]).astype(BF16)
    bd = bd_ref[...]

    def feature_major(j):
        return _dot_nt(wt_ref[j * width:(j + 1) * width, :], h)

    sbk_ref[...] = _dot(h, wk_ref[:, :width]).astype(BF16)
    k = _dot(h, wk_ref[:, width:])
    hi, lo = _split(k * k)
    ss = _dot(hi, bd) + _dot(lo, bd)
    dfk_ref[...] = ((k * lax.rsqrt(ss * (1.0 / HEAD_DIM) + NORM_EPS)) * gk_ref[...]).astype(BF16)

    sbq_ref[...] = (feature_major(0) * scale).astype(BF16)
    sbv_ref[...] = feature_major(1).astype(BF16)
    q = feature_major(2)
    hi, lo = _split(q * q)
    ss = _dot(bd, hi) + _dot(bd, lo)
    q = (q * lax.rsqrt(ss * (1.0 / HEAD_DIM) + NORM_EPS)) * gq_ref[...]
    dfq_ref[...] = (q * scale).astype(BF16)
    dfv_ref[...] = feature_major(3).astype(BF16)


def _proj_call(x, g, w_in, gq, gk, *, tm):
    n, d = x.shape
    width = w_in.shape[1] // 6
    cols = lambda j: w_in[:, j * width:(j + 1) * width]
    wk = jnp.concatenate([cols(1), cols(4)], axis=1).astype(BF16)
    wt = jnp.concatenate([cols(0), cols(2), cols(3), cols(5)], axis=1).T.astype(BF16)
    grp = lax.broadcasted_iota(jnp.int32, (width, width), 0) // HEAD_DIM
    bd = (grp == grp.T).astype(BF16)
    const = lambda a: pl.BlockSpec(a.shape, lambda i: (0, 0))
    tok_major = pl.BlockSpec((tm, width), lambda i: (i, 0))
    feat_major = pl.BlockSpec((width, tm), lambda i: (0, i))
    args = [x, g.reshape(1, d), wk, wt, bd,
            jnp.tile(gq, width // HEAD_DIM).reshape(width, 1),
            jnp.tile(gk, width // HEAD_DIM).reshape(1, width)]
    return pl.pallas_call(
        functools.partial(_proj_kernel, width=width),
        grid=(n // tm,),
        in_specs=[pl.BlockSpec((tm, d), lambda i: (i, 0))] + [const(a) for a in args[1:]],
        out_specs=[tok_major] * 2 + [feat_major] * 4,
        out_shape=[jax.ShapeDtypeStruct((n, width), BF16)] * 2
        + [jax.ShapeDtypeStruct((width, n), BF16)] * 4,
        compiler_params=_params(("parallel",)),
        name="proj",
    )(*args)


def _attn_specs(batch, seq, tq):
    nq = seq // tq
    q_spec = pl.BlockSpec((LANES, tq), lambda b, h, i: (h, b * nq + i))
    k_spec = pl.BlockSpec((seq, LANES), lambda b, h, i: (b, h))
    v_spec = pl.BlockSpec((LANES, seq), lambda b, h, i: (h, b))
    o_spec = pl.BlockSpec((tq, LANES), lambda b, h, i: (b * nq + i, h))
    return q_spec, k_spec, v_spec, o_spec


def _sb_kernel(q_ref, k_ref, v_ref, o_ref, acc_ref, l0_ref, l1_ref, *, tq):
    qi = pl.program_id(2)
    n_heads = LANES // HEAD_DIM
    l_refs = (l0_ref, l1_ref)
    key = lax.broadcasted_iota(jnp.int32, (tq, tq), 0)
    other = lax.broadcasted_iota(jnp.int32, (tq, tq), 1)
    later_keys = (other > key).astype(BF16)
    causal = key < other

    feat = lax.broadcasted_iota(jnp.int32, (LANES, tq), 0)
    q = q_ref[...]
    q_heads = [jnp.where((feat >= hh * HEAD_DIM) & (feat < (hh + 1) * HEAD_DIM), q,
                         jnp.zeros_like(q)) for hh in range(n_heads)]

    def block(hh, kb, carry_l, masked):
        feats = slice(hh * HEAD_DIM, (hh + 1) * HEAD_DIM)
        start = pl.multiple_of(kb * tq, tq)
        z = _dot(k_ref[pl.ds(start, tq), :], q_heads[hh])
        sp = jnp.maximum(z, 0.0) + jnp.log1p(jnp.exp(-jnp.abs(z)))
        log_keep = -sp
        if masked:
            log_keep = jnp.where(causal, log_keep, 0.0)
        hi, lo = _split(log_keep)
        later = _dot(later_keys, hi) + _dot(later_keys, lo)
        w = jnp.exp((z - sp) + later + carry_l)
        if masked:
            w = jnp.where(causal, w, 0.0)
        pv = _dot(v_ref[feats, pl.ds(start, tq)], w.astype(BF16))
        return pv, carry_l + later[:1, :] + log_keep[:1, :]

    lmax = []
    for hh in range(n_heads):
        pv, l_new = block(hh, qi, jnp.zeros((1, tq), F32), True)
        acc_ref[hh * HEAD_DIM:(hh + 1) * HEAD_DIM, :] = pv
        l_refs[hh][...] = l_new
        lmax.append(jnp.max(l_new))

    def cond(c):
        it, lm = c
        return jnp.logical_and(it < qi, lm > EXP_UNDERFLOW)

    def body(c):
        it, _ = c
        lm = []
        for hh in range(n_heads):
            pv, l_new = block(hh, qi - 1 - it, l_refs[hh][...], False)
            acc_ref[hh * HEAD_DIM:(hh + 1) * HEAD_DIM, :] += pv
            l_refs[hh][...] = l_new
            lm.append(jnp.max(l_new))
        return it + 1, jnp.maximum(lm[0], lm[1])

    lax.while_loop(cond, body, (jnp.int32(0), jnp.maximum(lmax[0], lmax[1])))
    o_ref[...] = acc_ref[...].T.astype(o_ref.dtype)


def _sb_call(q, k, v, *, batch, tq):
    n, width = k.shape
    q_spec, k_spec, v_spec, o_spec = _attn_specs(batch, n // batch, tq)
    return pl.pallas_call(
        functools.partial(_sb_kernel, tq=tq),
        grid=(batch, width // LANES, n // batch // tq),
        in_specs=[q_spec, k_spec, v_spec],
        out_specs=o_spec,
        out_shape=jax.ShapeDtypeStruct((n, width), BF16),
        scratch_shapes=[pltpu.VMEM((LANES, tq), F32)] + [pltpu.VMEM((1, tq), F32)] * 2,
        compiler_params=_params(("parallel", "parallel", "arbitrary")),
        name="sb_attn",
    )(q, k, v)


def _diff_kernel(slopes_ref, lam_ref, g_ref, q_ref, k_ref, v_ref, o_ref, bias_ref,
                 s0_ref, s1_ref, mb0_ref, mb1_ref, m0_ref, m1_ref, l0_ref, l1_ref,
                 acc0_ref, acc1_ref, *, tq, tk, lambda_init):
    h = pl.program_id(1)
    qi = pl.program_id(2)
    slope = slopes_ref[h]
    nsub = tk // tq
    s_refs, mb_refs = (s0_ref, s1_ref), (mb0_ref, mb1_ref)
    m_refs, l_refs, acc_refs = (m0_ref, m1_ref), (l0_ref, l1_ref), (acc0_ref, acc1_ref)

    @pl.when(qi == 0)
    def _():
        key = lax.broadcasted_iota(jnp.int32, (tk, tq), 0)
        query = lax.broadcasted_iota(jnp.int32, (tk, tq), 1)
        key_ahead = key - query
        rel_bias = slope * key_ahead.astype(F32)
        bias_ref[nsub] = rel_bias
        for r in range(nsub):
            bias_ref[r] = jnp.where(key_ahead <= r * tq, rel_bias, MASKED)

    feat = lax.broadcasted_iota(jnp.int32, (LANES, tq), 0)
    q = q_ref[...]
    q_maps = [jnp.where((feat >= c * HEAD_DIM) & (feat < (c + 1) * HEAD_DIM), q, jnp.zeros_like(q))
              for c in range(2)]

    for c in range(2):
        m_refs[c][...] = jnp.full_like(m_refs[c], MASKED)
        l_refs[c][...] = jnp.zeros_like(l_refs[c])
        acc_refs[c][...] = jnp.zeros_like(acc_refs[c])

    last = (qi * tq) // tk

    def offset(kb):
        return -slope * (qi * tq - kb * tk).astype(F32)

    def scores(c, kb):
        start = pl.multiple_of(kb * tk, tk)
        sel = jnp.where(kb == last, qi % nsub, nsub)
        s = _dot(k_ref[pl.ds(start, tk), :], q_maps[c]) + bias_ref[sel]
        s_refs[c][...] = s
        mb_refs[c][...] = jnp.max(s, axis=0, keepdims=True) + offset(kb)

    def accumulate(c, kb):
        start = pl.multiple_of(kb * tk, tk)
        m_old = m_refs[c][...]
        m_new = jnp.maximum(m_old, mb_refs[c][...])
        alpha = jnp.exp(m_old - m_new)
        p = jnp.exp(s_refs[c][...] - (m_new - offset(kb)))
        l_refs[c][...] = alpha * l_refs[c][...] + jnp.sum(p, axis=0, keepdims=True)
        acc_refs[c][...] = alpha * acc_refs[c][...] + _dot(v_ref[:, pl.ds(start, tk)],
                                                           p.astype(BF16))
        m_refs[c][...] = m_new

    scores(0, 0)

    def body(kb, carry):
        scores(1, kb)
        accumulate(0, kb)
        scores(0, jnp.minimum(kb + 1, last))
        accumulate(1, kb)
        return carry

    lax.fori_loop(0, last + 1, body, 0)

    lp = lam_ref[...]
    lam = (jnp.exp(jnp.sum(lp[0:1] * lp[1:2], axis=-1, keepdims=True))
           - jnp.exp(jnp.sum(lp[2:3] * lp[3:4], axis=-1, keepdims=True)) + lambda_init)
    o = (acc0_ref[...] * (1.0 / l0_ref[...])
         - lam * (acc1_ref[...] * (1.0 / l1_ref[...])))
    ms = jnp.mean(o * o, axis=0, keepdims=True)
    o = ((o * lax.rsqrt(ms + NORM_EPS)) * g_ref[...]) * (1.0 - lambda_init)
    o_ref[...] = o.T.astype(o_ref.dtype)


def _diff_call(q, k, v, lam_params, subln_g, *, batch, tq, tk, lambda_init):
    n, width = k.shape
    heads = width // LANES
    assert tk % tq == 0
    slopes = jnp.asarray([2.0 ** (-8.0 * (h + 1) / heads) for h in range(heads)], dtype=F32)
    q_spec, k_spec, v_spec, o_spec = _attn_specs(batch, n // batch, tq)
    return pl.pallas_call(
        functools.partial(_diff_kernel, tq=tq, tk=tk, lambda_init=lambda_init),
        grid=(batch, heads, n // batch // tq),
        in_specs=[pl.BlockSpec(memory_space=pltpu.SMEM),
                  pl.BlockSpec(lam_params.shape, lambda b, h, i: (0, 0)),
                  pl.BlockSpec((LANES, 1), lambda b, h, i: (0, 0)),
                  q_spec, k_spec, v_spec],
        out_specs=o_spec,
        out_shape=jax.ShapeDtypeStruct((n, width), BF16),
        scratch_shapes=[pltpu.VMEM((tk // tq + 1, tk, tq), F32)]
        + [pltpu.VMEM((tk, tq), F32)] * 2 + [pltpu.VMEM((1, tq), F32)] * 6
        + [pltpu.VMEM((LANES, tq), F32)] * 2,
        compiler_params=_params(("parallel", "parallel", "arbitrary")),
        name="diff_attn",
    )(slopes, lam_params.astype(F32), subln_g.reshape(LANES, 1).astype(F32), q, k, v)


def _merge_kernel(*refs, router):
    (x_ref, ysb_ref, ydf_ref, gmix_ref, wg_ref, bg_ref, wsb_ref, wdf_ref, wo_ref,
     gffn_ref) = refs[:10]
    rest = refs[10:]
    x = x_ref[...]
    d = x.shape[-1]
    h = _rms(x, gmix_ref[...]).astype(BF16)
    gates = jax.nn.sigmoid(_dot(h, wg_ref[...]) + bg_ref[...])
    merged = (gates[:, :d] * _dot(ysb_ref[...], wsb_ref[...])
              + gates[:, d:] * _dot(ydf_ref[...], wdf_ref[...]))
    xn = x + _dot(merged.astype(BF16), wo_ref[...])
    h2 = _rms(xn, gffn_ref[...])
    if router:
        wr_ref, xo_ref, h2_ref, lg_ref = rest
        lg_ref[...] = _dot(h2.astype(BF16), wr_ref[...])
    else:
        xo_ref, h2_ref = rest
    xo_ref[...] = xn
    h2_ref[...] = h2.astype(h2_ref.dtype)


def _merge_call(x, ysb, ydf, gmix, wg, bg, wsb, wdf, wo, gffn, w_router, *, tm):
    n, d = x.shape
    router = w_router is not None
    row = lambda w: pl.BlockSpec((tm, w), lambda i: (i, 0))
    const = lambda a: pl.BlockSpec(a.shape, lambda i: (0, 0))
    args = [x, ysb, ydf, gmix.reshape(1, d), wg, bg.reshape(1, -1), wsb, wdf, wo,
            gffn.reshape(1, d)]
    in_specs = [row(d), row(ysb.shape[1]), row(ydf.shape[1])] + [const(a) for a in args[3:]]
    out_specs = [row(d), row(d)]
    out_shape = [jax.ShapeDtypeStruct((n, d), F32),
                 jax.ShapeDtypeStruct((n, d), F32 if router else BF16)]
    if router:
        wr = jnp.zeros((d, LANES), BF16).at[:, :w_router.shape[1]].set(w_router.astype(BF16))
        args.append(wr)
        in_specs.append(const(wr))
        out_specs.append(row(LANES))
        out_shape.append(jax.ShapeDtypeStruct((n, LANES), F32))
    return pl.pallas_call(
        functools.partial(_merge_kernel, router=router),
        grid=(n // tm,),
        in_specs=in_specs,
        out_specs=out_specs,
        out_shape=out_shape,
        compiler_params=_params(("parallel",)),
        name="merge_router" if router else "merge",
    )(*args)


def _swiglu_kernel(x_ref, h_ref, wg_ref, wu_ref, wd_ref, o_ref, acc_ref):
    f = pl.program_id(1)

    @pl.when(f == 0)
    def _():
        acc_ref[...] = jnp.zeros_like(acc_ref)

    h = h_ref[...]
    g = _dot(h, wg_ref[...])
    u = _dot(h, wu_ref[...])
    act = (g * jax.nn.sigmoid(g)) * u
    acc_ref[...] += _dot(act.astype(BF16), wd_ref[...])

    @pl.when(f == pl.num_programs(1) - 1)
    def _():
        o_ref[...] = x_ref[...] + acc_ref[...]


def _swiglu_call(x, h, w_gate_up, w_down, *, tm, nf):
    n, d = x.shape
    d_ff = w_down.shape[0]
    tf = d_ff // nf
    assert tf * nf == d_ff and tf % LANES == 0
    row = pl.BlockSpec((tm, d), lambda i, f: (i, 0))
    return pl.pallas_call(
        _swiglu_kernel,
        grid=(n // tm, nf),
        in_specs=[row, row,
                  pl.BlockSpec((d, tf), lambda i, f: (0, f)),
                  pl.BlockSpec((d, tf), lambda i, f: (0, f + nf)),
                  pl.BlockSpec((tf, d), lambda i, f: (f, 0))],
        out_specs=row,
        out_shape=jax.ShapeDtypeStruct((n, d), F32),
        scratch_shapes=[pltpu.VMEM((tm, d), F32)],
        compiler_params=_params(("parallel", "arbitrary")),
        name="swiglu",
    )(x, h, w_gate_up, w_gate_up, w_down)


def _expert_kernel(be_ref, tok_ref, dst_ref, meta_ref, h_hbm, sw_ref, w1_ref, w3_ref, w2_ref,
                   y_hbm, xbuf, xb_ref, acc_ref, ybuf, zrow_ref, gsem, ssem, *, n_spare):
    del be_ref
    i = pl.program_id(0)
    f = pl.program_id(1)
    nf = pl.num_programs(1)
    nused = meta_ref[0]
    rows = xb_ref.shape[0]

    def gather_copy(blk, r):
        return pltpu.make_async_copy(h_hbm.at[pl.ds(tok_ref[blk * rows + r], 1)],
                                     xbuf.at[blk % 2, pl.ds(r, 1)], gsem.at[blk % 2])

    def scatter_copy(blk, r):
        return pltpu.make_async_copy(ybuf.at[(blk + 2) % 2, pl.ds(r, 1)],
                                     y_hbm.at[pl.ds(dst_ref[(blk + 1) * rows + r], 1)],
                                     ssem.at[(blk + 2) % 2])

    def for_rows(lo, n, fn):
        def body(r, carry):
            fn(lo + r)
            return carry
        lax.fori_loop(0, n, body, 0, unroll=8)

    @pl.when(jnp.logical_and(i == 0, f == 0))
    def _():
        ybuf[1] = jnp.zeros(ybuf.shape[1:], ybuf.dtype)
        for_rows(0, rows, lambda r: gather_copy(0, r).start())

    @pl.when(jnp.logical_and(f == 0, i <= nused))
    def _():
        for_rows(0, rows, lambda r: gather_copy(i, r).wait())

    @pl.when(jnp.logical_and(f == 0, i < nused))
    def _():
        xb_ref[...] = xbuf[i % 2].astype(BF16)
        acc_ref[...] = jnp.zeros_like(acc_ref)

    chunk = rows // nf

    @pl.when(i < nused)
    def _():
        for r in range(chunk):
            gather_copy(i + 1, f * chunk + r).start()
            scatter_copy(i - 1, f * chunk + r).start()
        xb = xb_ref[...]
        g = _dot(xb, w1_ref[...])
        u = _dot(xb, w3_ref[...])
        hid = (g * jax.nn.sigmoid(g)) * u
        acc_ref[...] += _dot(hid.astype(BF16), w2_ref[...])

    @pl.when(i == nused)
    def _():
        for_rows(f * chunk, chunk, lambda r: scatter_copy(i - 1, r).start())

    @pl.when(f == nf - 1)
    def _():
        @pl.when(jnp.logical_and(i < nused, i >= 1))
        def _():
            for_rows(0, rows, lambda r: scatter_copy(i - 2, r).wait())

        @pl.when(i < nused)
        def _():
            ybuf[i % 2] = acc_ref[...] * sw_ref[...]

        @pl.when(i == nused)
        def _():
            @pl.when(i >= 1)
            def _():
                for_rows(0, rows, lambda r: scatter_copy(i - 2, r).wait())
            for_rows(0, rows, lambda r: scatter_copy(i - 1, r).wait())

            zrow_ref[...] = jnp.zeros_like(zrow_ref)
            n_pad = meta_ref[1]
            spare_lo = y_hbm.shape[0] - rows - n_spare

            def zero_copy(r):
                return pltpu.make_async_copy(zrow_ref.at[pl.ds(0, 1)],
                                             y_hbm.at[pl.ds(spare_lo + r, 1)], ssem.at[0])

            def fill(r, carry):
                zero_copy(r).start()
                return carry

            def drain(r, carry):
                zero_copy(r).wait()
                return carry

            lax.fori_loop(n_pad, n_spare, fill, 0)
            lax.fori_loop(n_pad, n_spare, drain, 0)


def _expert_call(block_e, slot_tok, dst_all, meta, h, slot_w, w1, w3, w2, *, n_spare, nf):
    n_slots = slot_tok.shape[0]
    n_out = h.shape[0] * TOP_K + n_spare + MOE_BLOCK
    n_blocks = n_slots // MOE_BLOCK
    d = h.shape[1]
    d_ff = w1.shape[2]
    tf = d_ff // nf
    assert tf * nf == d_ff and tf % LANES == 0 and MOE_BLOCK % nf == 0
    grid_spec = pltpu.PrefetchScalarGridSpec(
        num_scalar_prefetch=4,
        grid=(n_blocks, nf),
        in_specs=[pl.BlockSpec(memory_space=pl.ANY),
                  pl.BlockSpec((MOE_BLOCK, 1), lambda i, f, be, *_: (i, 0)),
                  pl.BlockSpec((None, d, tf), lambda i, f, be, *_: (be[i], 0, f)),
                  pl.BlockSpec((None, d, tf), lambda i, f, be, *_: (be[i], 0, f)),
                  pl.BlockSpec((None, tf, d), lambda i, f, be, *_: (be[i], f, 0))],
        out_specs=pl.BlockSpec(memory_space=pl.ANY),
        scratch_shapes=[pltpu.VMEM((2, MOE_BLOCK, d), F32),
                        pltpu.VMEM((MOE_BLOCK, d), BF16),
                        pltpu.VMEM((MOE_BLOCK, d), F32),
                        pltpu.VMEM((2, MOE_BLOCK, d), F32),
                        pltpu.VMEM((8, d), F32),
                        pltpu.SemaphoreType.DMA((2,)),
                        pltpu.SemaphoreType.DMA((2,))],
    )
    return pl.pallas_call(
        functools.partial(_expert_kernel, n_spare=n_spare),
        grid_spec=grid_spec,
        out_shape=jax.ShapeDtypeStruct((n_out, d), F32),
        compiler_params=_params(("arbitrary", "arbitrary")),
        name="moe_experts",
    )(block_e, slot_tok, dst_all, meta, h, slot_w, w1, w3, w2)


def _combine_kernel(x_ref, y0_ref, y1_ref, o_ref):
    o_ref[...] = x_ref[...] + (y0_ref[...] + y1_ref[...])


def _combine_call(x, y, *, tm):
    n, d = x.shape
    steps = n // tm
    return pl.pallas_call(
        _combine_kernel,
        grid=(steps,),
        in_specs=[pl.BlockSpec((tm, d), lambda i: (i, 0)),
                  pl.BlockSpec((tm, d), lambda i: (i, 0)),
                  pl.BlockSpec((tm, d), lambda i: (i + steps, 0))],
        out_specs=pl.BlockSpec((tm, d), lambda i: (i, 0)),
        out_shape=jax.ShapeDtypeStruct((n, d), F32),
        compiler_params=_params(("parallel",)),
        name="moe_combine",
    )(x, y, y)


def _route(logits):
    n_tok = logits.shape[0]
    n_assign = n_tok * TOP_K
    n_slots = -(-n_assign // MOE_BLOCK) * MOE_BLOCK + N_EXPERTS * MOE_BLOCK
    n_blocks = n_slots // MOE_BLOCK
    top_logit, top_idx = lax.top_k(logits, TOP_K)
    top_w = jax.nn.softmax(top_logit, axis=-1)
    onehot = jnp.sum(jax.nn.one_hot(top_idx, N_EXPERTS, dtype=jnp.int32), axis=1)
    csum = jnp.cumsum(onehot, axis=0)
    counts = csum[-1]
    padded = (counts + MOE_BLOCK - 1) // MOE_BLOCK * MOE_BLOCK
    padded_end = jnp.cumsum(padded)
    padded_start = padded_end - padded
    rank = jnp.take_along_axis(csum - onehot, top_idx, axis=1)
    slots = (padded_start[top_idx] + rank).astype(jnp.int32)
    dst = (jnp.arange(TOP_K, dtype=jnp.int32)[None, :] * n_tok
           + jnp.arange(n_tok, dtype=jnp.int32)[:, None])
    slot_dst = jnp.full((n_slots,), -1, jnp.int32).at[slots.reshape(-1)].set(dst.reshape(-1))
    real = slot_dst >= 0
    n_spare = n_slots - n_assign
    pad_rank = jnp.cumsum(jnp.logical_not(real).astype(jnp.int32)) - 1
    slot_tok = jnp.where(real, slot_dst % n_tok, 0)
    w_kmajor = top_w.T.reshape(-1)
    slot_w = jnp.where(real, w_kmajor[jnp.maximum(slot_dst, 0)], 0.0)
    slot_dst = jnp.where(real, slot_dst, n_assign + pad_rank)
    virtual = n_assign + n_spare + jnp.arange(MOE_BLOCK, dtype=jnp.int32)
    dst_all = jnp.concatenate([virtual, slot_dst])
    block_e = jnp.minimum(
        jnp.searchsorted(padded_end, jnp.arange(n_blocks) * MOE_BLOCK, side='right'),
        N_EXPERTS - 1).astype(jnp.int32)
    meta = jnp.stack([padded_end[-1] // MOE_BLOCK, padded_end[-1] - n_assign]).astype(jnp.int32)
    return block_e, slot_tok, dst_all, meta, slot_w.reshape(n_slots, 1), n_spare


def kernel(x, norm_mix_g, w_in, diff_q_norm_g, diff_k_norm_g, diff_lambda, diff_subln_g,
           w_gate, b_gate, w_branch_sb, w_branch_diff, w_out, norm_ffn_g,
           ffn_w_gate_up, ffn_w_down, moe_w_router, moe_w1, moe_w3, moe_w2):
    batch, seq, d = x.shape
    n = batch * seq
    depth = w_in.shape[0]
    tm = min(512, n)
    tq = min(256, seq)
    xf = x.reshape(n, d).astype(F32)
    bf = lambda w: w.astype(BF16)
    for i in range(depth):
        sbk, dfk, sbq, sbv, dfq, dfv = _proj_call(
            xf, norm_mix_g[i], w_in[i], diff_q_norm_g[i], diff_k_norm_g[i], tm=tm)
        y_sb = _sb_call(sbq, sbk, sbv, batch=batch, tq=tq)
        lambda_init = 0.8 - 0.6 * math.exp(-0.3 * i)
        y_df = _diff_call(dfq, dfk, dfv, diff_lambda[i], diff_subln_g[i],
                          batch=batch, tq=tq, tk=min(1024, seq), lambda_init=lambda_init)
        j = i // 2
        dense = i % 2 == 0
        outs = _merge_call(xf, y_sb, y_df, norm_mix_g[i], bf(w_gate[i]), b_gate[i],
                           bf(w_branch_sb[i]), bf(w_branch_diff[i]), bf(w_out[i]),
                           norm_ffn_g[i], None if dense else moe_w_router[j], tm=tm)
        if dense:
            xf, h2 = outs
            xf = _swiglu_call(xf, h2, bf(ffn_w_gate_up[j]), bf(ffn_w_down[j]), tm=tm, nf=2)
        else:
            xf, h2, logits = outs
            block_e, slot_tok, dst_all, meta, slot_w, n_spare = _route(logits[:, :N_EXPERTS])
            y = _expert_call(block_e, slot_tok, dst_all, meta, h2, slot_w,
                             bf(moe_w1[j]), bf(moe_w3[j]), bf(moe_w2[j]), n_spare=n_spare, nf=4)
            xf = _combine_call(xf, y, tm=tm)
    return xf.reshape(batch, seq, d).astype(x.dtype)
```

```python
import functools
import math

import jax
import jax.numpy as jnp
from jax import lax
from jax.experimental import pallas as pl
from jax.experimental.pallas import tpu as pltpu

F32 = jnp.float32
BF16 = jnp.bfloat16

HEAD_DIM = 64
NORM_EPS = 1e-6
N_EXPERTS = 8
TOP_K = 2
MOE_BLOCK = 512
LANES = 128
VMEM_LIMIT = 56 * 1024 * 1024

EXP_UNDERFLOW = -104.0
EXP2_UNDERFLOW = 152.0
MASKED = -1e30
LOG2E = math.log2(math.e)
V_ROWS = LANES + 16


def _dot(a, b):
    return jnp.dot(a, b, preferred_element_type=F32)


def _dot_nt(a, b):
    return lax.dot_general(a, b, (((1,), (1,)), ((), ())), preferred_element_type=F32)


def _rms(x, g):
    ms = jnp.mean(x * x, axis=-1, keepdims=True)
    return (x * lax.rsqrt(ms + NORM_EPS)) * g


def _split(x):
    hi = x.astype(BF16)
    return hi, (x - hi.astype(F32)).astype(BF16)


def _params(sem, vmem=VMEM_LIMIT):
    return pltpu.CompilerParams(dimension_semantics=sem, vmem_limit_bytes=vmem)


def _proj_kernel(x_ref, g_ref, wk_ref, wt_ref, bd_ref, gq_ref, gk_ref,
                 sbk_ref, dfk_ref, sbq_ref, sbv_ref, dfq_ref, dfv_ref, *, width):
    scale = HEAD_DIM ** -0.5
    h = _rms(x_ref[...], g_ref[...]).astype(BF16)
    bd = bd_ref[...]

    def feature_major(j):
        return _dot_nt(wt_ref[j * width:(j + 1) * width, :], h)

    sbk_ref[...] = _dot(h, wk_ref[:, :width]).astype(BF16)
    k = _dot(h, wk_ref[:, width:])
    hi, lo = _split(k * k)
    ss = _dot(hi, bd) + _dot(lo, bd)
    dfk_ref[...] = ((k * lax.rsqrt(ss * (1.0 / HEAD_DIM) + NORM_EPS)) * gk_ref[...]).astype(BF16)

    sbq_ref[...] = (feature_major(0) * scale).astype(BF16)
    sbv_ref[...] = feature_major(1).astype(BF16)
    q = feature_major(2)
    hi, lo = _split(q * q)
    ss = _dot(bd, hi) + _dot(bd, lo)
    q = (q * lax.rsqrt(ss * (1.0 / HEAD_DIM) + NORM_EPS)) * gq_ref[...]
    dfq_ref[...] = (q * (scale * LOG2E)).astype(BF16)
    v = feature_major(3).astype(BF16)
    for hd in range(width // LANES):
        dfv_ref[hd * V_ROWS:hd * V_ROWS + LANES, :] = v[hd * LANES:(hd + 1) * LANES, :]
        dfv_ref[hd * V_ROWS + LANES:(hd + 1) * V_ROWS, :] = jnp.ones(
            (V_ROWS - LANES, v.shape[1]), BF16)


def _proj_call(x, g, w_in, gq, gk, *, tm):
    n, d = x.shape
    width = w_in.shape[1] // 6
    cols = lambda j: w_in[:, j * width:(j + 1) * width]
    wk = jnp.concatenate([cols(1), cols(4)], axis=1).astype(BF16)
    wt = jnp.concatenate([cols(0), cols(2), cols(3), cols(5)], axis=1).T.astype(BF16)
    grp = lax.broadcasted_iota(jnp.int32, (width, width), 0) // HEAD_DIM
    bd = (grp == grp.T).astype(BF16)
    const = lambda a: pl.BlockSpec(a.shape, lambda i: (0, 0))
    tok_major = pl.BlockSpec((tm, width), lambda i: (i, 0))
    feat_major = pl.BlockSpec((width, tm), lambda i: (0, i))
    v_rows = width // LANES * V_ROWS
    args = [x, g.reshape(1, d), wk, wt, bd,
            jnp.tile(gq, width // HEAD_DIM).reshape(width, 1),
            jnp.tile(gk, width // HEAD_DIM).reshape(1, width)]
    return pl.pallas_call(
        functools.partial(_proj_kernel, width=width),
        grid=(n // tm,),
        in_specs=[pl.BlockSpec((tm, d), lambda i: (i, 0))] + [const(a) for a in args[1:]],
        out_specs=[tok_major] * 2 + [feat_major] * 3 + [pl.BlockSpec((v_rows, tm), lambda i: (0, i))],
        out_shape=[jax.ShapeDtypeStruct((n, width), BF16)] * 2
        + [jax.ShapeDtypeStruct((width, n), BF16)] * 3 + [jax.ShapeDtypeStruct((v_rows, n), BF16)],
        compiler_params=_params(("parallel",)),
        name="proj",
    )(*args)


def _attn_specs(batch, seq, tq):
    nq = seq // tq
    q_spec = pl.BlockSpec((LANES, tq), lambda b, h, i: (h, b * nq + i))
    k_spec = pl.BlockSpec((seq, LANES), lambda b, h, i: (b, h))
    v_spec = pl.BlockSpec((LANES, seq), lambda b, h, i: (h, b))
    o_spec = pl.BlockSpec((tq, LANES), lambda b, h, i: (b * nq + i, h))
    return q_spec, k_spec, v_spec, o_spec


def _sb_kernel(q_ref, k_ref, v_ref, o_ref, acc_ref, l0_ref, l1_ref, row0_ref, row1_ref,
               lb0_ref, lb1_ref, hi0_ref, hi1_ref, lo0_ref, lo1_ref, mask_ref, later_ref, *, tq):
    qi = pl.program_id(2)
    n_heads = LANES // HEAD_DIM
    l_refs, lb_refs, row_refs = (l0_ref, l1_ref), (lb0_ref, lb1_ref), (row0_ref, row1_ref)
    hi_refs, lo_refs = (hi0_ref, hi1_ref), (lo0_ref, lo1_ref)

    @pl.when(qi == 0)
    def _():
        key = lax.broadcasted_iota(jnp.int32, (tq, tq), 0)
        other = lax.broadcasted_iota(jnp.int32, (tq, tq), 1)
        mask_ref[0] = jnp.where(key < other, 1.0, 0.0).astype(F32)
        mask_ref[1] = jnp.ones((tq, tq), F32)
        later_ref[...] = jnp.where(other > key, 1.0, 0.0).astype(BF16)

    feat = lax.broadcasted_iota(jnp.int32, (LANES, tq), 0)
    q = q_ref[...]
    q_heads = [jnp.where((feat >= hh * HEAD_DIM) & (feat < (hh + 1) * HEAD_DIM), q,
                         jnp.zeros_like(q)) for hh in range(n_heads)]

    acc_ref[...] = jnp.zeros_like(acc_ref)
    for hh in range(n_heads):
        l_refs[hh][...] = jnp.zeros_like(l_refs[hh])

    def logits(hh, kb):
        start = pl.multiple_of(kb * tq, tq)
        z = _dot(k_ref[pl.ds(start, tq), :], q_heads[hh])
        sp = jnp.maximum(z, 0.0) + jnp.log1p(jnp.exp(-jnp.abs(z)))
        log_keep = -sp * mask_ref[jnp.where(kb == qi, 0, 1)]
        hi, lo = _split(log_keep)
        hi_refs[hh][...] = hi
        lo_refs[hh][...] = lo
        lb_refs[hh][...] = z - sp
        row_refs[hh][...] = log_keep[:1, :]

    def weights(hh, kb):
        feats = slice(hh * HEAD_DIM, (hh + 1) * HEAD_DIM)
        start = pl.multiple_of(kb * tq, tq)
        later_keys = later_ref[...]
        later = _dot(later_keys, hi_refs[hh][...]) + _dot(later_keys, lo_refs[hh][...])
        carry_l = l_refs[hh][...]
        w = jnp.exp(lb_refs[hh][...] + later + carry_l) * mask_ref[jnp.where(kb == qi, 0, 1)]
        acc_ref[feats, :] += _dot(v_ref[feats, pl.ds(start, tq)], w.astype(BF16))
        l_new = carry_l + later[:1, :] + row_refs[hh][...]
        l_refs[hh][...] = l_new
        return jnp.max(l_new)

    logits(0, qi)

    def cond(c):
        it, lm = c
        return jnp.logical_and(it <= qi, lm > EXP_UNDERFLOW)

    def body(c):
        it, _ = c
        kb = qi - it
        logits(1, kb)
        lm0 = weights(0, kb)
        logits(0, jnp.maximum(kb - 1, 0))
        lm1 = weights(1, kb)
        return it + 1, jnp.maximum(lm0, lm1)

    lax.while_loop(cond, body, (jnp.int32(0), jnp.float32(0.0)))
    o_ref[...] = acc_ref[...].T.astype(o_ref.dtype)


def _sb_call(q, k, v, *, batch, tq):
    n, width = k.shape
    q_spec, k_spec, v_spec, o_spec = _attn_specs(batch, n // batch, tq)
    return pl.pallas_call(
        functools.partial(_sb_kernel, tq=tq),
        grid=(batch, width // LANES, n // batch // tq),
        in_specs=[q_spec, k_spec, v_spec],
        out_specs=o_spec,
        out_shape=jax.ShapeDtypeStruct((n, width), BF16),
        scratch_shapes=[pltpu.VMEM((LANES, tq), F32)] + [pltpu.VMEM((1, tq), F32)] * 4
        + [pltpu.VMEM((tq, tq), F32)] * 2 + [pltpu.VMEM((tq, tq), BF16)] * 4
        + [pltpu.VMEM((2, tq, tq), F32), pltpu.VMEM((tq, tq), BF16)],
        compiler_params=_params(("parallel", "parallel", "arbitrary")),
        name="sb_attn",
    )(q, k, v)


def _diff_kernel(slopes_ref, lam_ref, g_ref, q_ref, k_ref, v_ref, o_ref, bias_ref,
                 s0_ref, s1_ref, mb0_ref, mb1_ref, m0_ref, m1_ref, acc0_ref, acc1_ref, kmax_ref,
                 *, tq, tk, lambda_init):
    h = pl.program_id(1)
    qi = pl.program_id(2)
    slope = slopes_ref[0, h]
    inv_slope = slopes_ref[1, h]
    nsub = tk // tq
    s_refs, mb_refs = (s0_ref, s1_ref), (mb0_ref, mb1_ref)
    m_refs, acc_refs = (m0_ref, m1_ref), (acc0_ref, acc1_ref)

    @pl.when(qi == 0)
    def _():
        key = lax.broadcasted_iota(jnp.int32, (tk, tq), 0)
        query = lax.broadcasted_iota(jnp.int32, (tk, tq), 1)
        key_ahead = key - query
        rel_bias = slope * key_ahead.astype(F32)
        bias_ref[nsub] = rel_bias
        for r in range(nsub):
            bias_ref[r] = jnp.where(key_ahead <= r * tq, rel_bias, MASKED)
        kf = k_ref[...].astype(F32)
        kmax_ref[0] = jnp.max(jnp.sum(kf * kf, axis=1, keepdims=True))

    feat = lax.broadcasted_iota(jnp.int32, (LANES, tq), 0)
    q = q_ref[...]
    q_maps = [jnp.where((feat >= c * HEAD_DIM) & (feat < (c + 1) * HEAD_DIM), q, jnp.zeros_like(q))
              for c in range(2)]

    for c in range(2):
        m_refs[c][...] = jnp.full_like(m_refs[c], MASKED)
        acc_refs[c][...] = jnp.zeros_like(acc_refs[c])

    last = (qi * tq) // tk

    qf = q.astype(F32)
    qk = jnp.max(jnp.sqrt(jnp.sum(qf * qf, axis=0, keepdims=True) * kmax_ref[0]))
    skip_dist = (EXP2_UNDERFLOW + 2.01 * qk) * inv_slope
    first = jnp.clip(((qi * tq).astype(F32) - skip_dist) * (1.0 / tk), 0.0, last.astype(F32))
    first = first.astype(jnp.int32)

    def offset(kb):
        return -slope * (qi * tq - kb * tk).astype(F32)

    half = tk // 2

    def scores(c, kb):
        sel = jnp.where(kb == last, qi % nsub, nsub)
        mb = None
        for part in range(2):
            rows = pl.ds(part * half, half)
            start = pl.multiple_of(kb * tk + part * half, half)
            s = _dot(k_ref[pl.ds(start, half), :], q_maps[c]) + bias_ref[sel, rows, :]
            s_refs[c][rows, :] = s
            part_max = jnp.max(s, axis=0, keepdims=True)
            mb = part_max if mb is None else jnp.maximum(mb, part_max)
        mb_refs[c][...] = mb + offset(kb)

    def accumulate(c, kb):
        m_old = m_refs[c][...]
        m_new = jnp.maximum(m_old, mb_refs[c][...])
        alpha = jnp.exp2(m_old - m_new)
        shift = m_new - offset(kb)
        pv = None
        for part in range(2):
            start = pl.multiple_of(kb * tk + part * half, half)
            p = jnp.exp2(s_refs[c][pl.ds(part * half, half), :] - shift)
            part_pv = _dot(v_ref[:, pl.ds(start, half)], p.astype(BF16))
            pv = part_pv if pv is None else pv + part_pv
        acc_refs[c][...] = alpha * acc_refs[c][...] + pv
        m_refs[c][...] = m_new

    scores(0, first)

    def body(kb, carry):
        scores(1, kb)
        accumulate(0, kb)
        scores(0, jnp.minimum(kb + 1, last))
        accumulate(1, kb)
        return carry

    lax.fori_loop(first, last + 1, body, 0)

    lp = lam_ref[...]
    lam = (jnp.exp(jnp.sum(lp[0:1] * lp[1:2], axis=-1, keepdims=True))
           - jnp.exp(jnp.sum(lp[2:3] * lp[3:4], axis=-1, keepdims=True)) + lambda_init)
    a0, a1 = acc0_ref[...], acc1_ref[...]
    o = (a0[:LANES] * (1.0 / a0[LANES:LANES + 1])
         - lam * (a1[:LANES] * (1.0 / a1[LANES:LANES + 1])))
    ms = jnp.mean(o * o, axis=0, keepdims=True)
    o = ((o * lax.rsqrt(ms + NORM_EPS)) * g_ref[...]) * (1.0 - lambda_init)
    o_ref[...] = o.T.astype(o_ref.dtype)


def _diff_call(q, k, v, lam_params, subln_g, *, batch, tq, tk, lambda_init):
    n, width = k.shape
    heads = width // LANES
    assert tk % tq == 0
    slopes = [2.0 ** (-8.0 * (h + 1) / heads) * LOG2E for h in range(heads)]
    slopes = jnp.asarray([slopes, [1.0 / s for s in slopes]], dtype=F32)
    seq = n // batch
    q_spec, k_spec, _, o_spec = _attn_specs(batch, seq, tq)
    v_spec = pl.BlockSpec((V_ROWS, seq), lambda b, h, i: (h, b))
    return pl.pallas_call(
        functools.partial(_diff_kernel, tq=tq, tk=tk, lambda_init=lambda_init),
        grid=(batch, heads, seq // tq),
        in_specs=[pl.BlockSpec(memory_space=pltpu.SMEM),
                  pl.BlockSpec(lam_params.shape, lambda b, h, i: (0, 0)),
                  pl.BlockSpec((LANES, 1), lambda b, h, i: (0, 0)),
                  q_spec, k_spec, v_spec],
        out_specs=o_spec,
        out_shape=jax.ShapeDtypeStruct((n, width), BF16),
        scratch_shapes=[pltpu.VMEM((tk // tq + 1, tk, tq), F32)]
        + [pltpu.VMEM((tk, tq), F32)] * 2 + [pltpu.VMEM((1, tq), F32)] * 4
        + [pltpu.VMEM((V_ROWS, tq), F32)] * 2 + [pltpu.SMEM((1,), F32)],
        compiler_params=_params(("parallel", "parallel", "arbitrary")),
        name="diff_attn",
    )(slopes, lam_params.astype(F32), subln_g.reshape(LANES, 1).astype(F32), q, k, v)


def _merge_kernel(*refs, router):
    (x_ref, ysb_ref, ydf_ref, gmix_ref, wg_ref, bg_ref, wsb_ref, wdf_ref, wo_ref,
     gffn_ref) = refs[:10]
    rest = refs[10:]
    x = x_ref[...]
    d = x.shape[-1]
    h = _rms(x, gmix_ref[...]).astype(BF16)
    gates = jax.nn.sigmoid(_dot(h, wg_ref[...]) + bg_ref[...])
    merged = (gates[:, :d] * _dot(ysb_ref[...], wsb_ref[...])
              + gates[:, d:] * _dot(ydf_ref[...], wdf_ref[...]))
    xn = x + _dot(merged.astype(BF16), wo_ref[...])
    h2 = _rms(xn, gffn_ref[...])
    if router:
        wr_ref, xo_ref, h2_ref, lg_ref = rest
        lg_ref[...] = _dot(h2.astype(BF16), wr_ref[...])
    else:
        xo_ref, h2_ref = rest
    xo_ref[...] = xn
    h2_ref[...] = h2.astype(h2_ref.dtype)


def _merge_call(x, ysb, ydf, gmix, wg, bg, wsb, wdf, wo, gffn, w_router, *, tm):
    n, d = x.shape
    router = w_router is not None
    row = lambda w: pl.BlockSpec((tm, w), lambda i: (i, 0))
    const = lambda a: pl.BlockSpec(a.shape, lambda i: (0, 0))
    args = [x, ysb, ydf, gmix.reshape(1, d), wg, bg.reshape(1, -1), wsb, wdf, wo,
            gffn.reshape(1, d)]
    in_specs = [row(d), row(ysb.shape[1]), row(ydf.shape[1])] + [const(a) for a in args[3:]]
    out_specs = [row(d), row(d)]
    out_shape = [jax.ShapeDtypeStruct((n, d), F32),
                 jax.ShapeDtypeStruct((n, d), F32 if router else BF16)]
    if router:
        wr = jnp.zeros((d, LANES), BF16).at[:, :w_router.shape[1]].set(w_router.astype(BF16))
        args.append(wr)
        in_specs.append(const(wr))
        out_specs.append(row(LANES))
        out_shape.append(jax.ShapeDtypeStruct((n, LANES), F32))
    return pl.pallas_call(
        functools.partial(_merge_kernel, router=router),
        grid=(n // tm,),
        in_specs=in_specs,
        out_specs=out_specs,
        out_shape=out_shape,
        compiler_params=_params(("parallel",)),
        name="merge_router" if router else "merge",
    )(*args)


def _swiglu_kernel(x_ref, h_ref, wg_ref, wu_ref, wd_ref, o_ref, acc_ref):
    f = pl.program_id(1)

    @pl.when(f == 0)
    def _():
        acc_ref[...] = jnp.zeros_like(acc_ref)

    h = h_ref[...]
    g = _dot(h, wg_ref[...])
    u = _dot(h, wu_ref[...])
    act = (g * jax.nn.sigmoid(g)) * u
    acc_ref[...] += _dot(act.astype(BF16), wd_ref[...])

    @pl.when(f == pl.num_programs(1) - 1)
    def _():
        o_ref[...] = x_ref[...] + acc_ref[...]


def _swiglu_call(x, h, w_gate_up, w_down, *, tm, nf):
    n, d = x.shape
    d_ff = w_down.shape[0]
    tf = d_ff // nf
    assert tf * nf == d_ff and tf % LANES == 0
    row = pl.BlockSpec((tm, d), lambda i, f: (i, 0))
    return pl.pallas_call(
        _swiglu_kernel,
        grid=(n // tm, nf),
        in_specs=[row, row,
                  pl.BlockSpec((d, tf), lambda i, f: (0, f)),
                  pl.BlockSpec((d, tf), lambda i, f: (0, f + nf)),
                  pl.BlockSpec((tf, d), lambda i, f: (f, 0))],
        out_specs=row,
        out_shape=jax.ShapeDtypeStruct((n, d), F32),
        scratch_shapes=[pltpu.VMEM((tm, d), F32)],
        compiler_params=_params(("parallel", "arbitrary")),
        name="swiglu",
    )(x, h, w_gate_up, w_gate_up, w_down)


def _expert_kernel(be_ref, tok_ref, dst_ref, meta_ref, h_hbm, sw_ref, w1_ref, w3_ref, w2_ref,
                   y_hbm, xbuf, xb_ref, acc_ref, ybuf, zrow_ref, gsem, ssem, *, n_spare):
    del be_ref
    i = pl.program_id(0)
    f = pl.program_id(1)
    nf = pl.num_programs(1)
    nused = meta_ref[0]
    rows = xb_ref.shape[0]

    def gather_copy(blk, r):
        return pltpu.make_async_copy(h_hbm.at[pl.ds(tok_ref[blk * rows + r], 1)],
                                     xbuf.at[blk % 2, pl.ds(r, 1)], gsem.at[blk % 2])

    def scatter_copy(blk, r):
        return pltpu.make_async_copy(ybuf.at[(blk + 2) % 2, pl.ds(r, 1)],
                                     y_hbm.at[pl.ds(dst_ref[(blk + 1) * rows + r], 1)],
                                     ssem.at[(blk + 2) % 2])

    def for_rows(lo, n, fn):
        def body(r, carry):
            fn(lo + r)
            return carry
        lax.fori_loop(0, n, body, 0, unroll=8)

    @pl.when(jnp.logical_and(i == 0, f == 0))
    def _():
        ybuf[1] = jnp.zeros(ybuf.shape[1:], ybuf.dtype)
        for_rows(0, rows, lambda r: gather_copy(0, r).start())

    @pl.when(jnp.logical_and(f == 0, i <= nused))
    def _():
        for_rows(0, rows, lambda r: gather_copy(i, r).wait())

    @pl.when(jnp.logical_and(f == 0, i < nused))
    def _():
        xb_ref[...] = xbuf[i % 2].astype(BF16)
        acc_ref[...] = jnp.zeros_like(acc_ref)

    chunk = rows // nf

    @pl.when(i < nused)
    def _():
        for r in range(chunk):
            gather_copy(i + 1, f * chunk + r).start()
            scatter_copy(i - 1, f * chunk + r).start()
        xb = xb_ref[...]
        g = _dot(xb, w1_ref[...])
        u = _dot(xb, w3_ref[...])
        hid = (g * jax.nn.sigmoid(g)) * u
        acc_ref[...] += _dot(hid.astype(BF16), w2_ref[...])

    @pl.when(i == nused)
    def _():
        for_rows(f * chunk, chunk, lambda r: scatter_copy(i - 1, r).start())

    @pl.when(f == nf - 1)
    def _():
        @pl.when(jnp.logical_and(i < nused, i >= 1))
        def _():
            for_rows(0, rows, lambda r: scatter_copy(i - 2, r).wait())

        @pl.when(i < nused)
        def _():
            ybuf[i % 2] = acc_ref[...] * sw_ref[...]

        @pl.when(i == nused)
        def _():
            @pl.when(i >= 1)
            def _():
                for_rows(0, rows, lambda r: scatter_copy(i - 2, r).wait())
            for_rows(0, rows, lambda r: scatter_copy(i - 1, r).wait())

            zrow_ref[...] = jnp.zeros_like(zrow_ref)
            n_pad = meta_ref[1]
            spare_lo = y_hbm.shape[0] - rows - n_spare

            def zero_copy(r):
                return pltpu.make_async_copy(zrow_ref.at[pl.ds(0, 1)],
                                             y_hbm.at[pl.ds(spare_lo + r, 1)], ssem.at[0])

            def fill(r, carry):
                zero_copy(r).start()
                return carry

            def drain(r, carry):
                zero_copy(r).wait()
                return carry

            lax.fori_loop(n_pad, n_spare, fill, 0)
            lax.fori_loop(n_pad, n_spare, drain, 0)


def _expert_call(block_e, slot_tok, dst_all, meta, h, slot_w, w1, w3, w2, *, n_spare, nf):
    n_slots = slot_tok.shape[0]
    n_out = h.shape[0] * TOP_K + n_spare + MOE_BLOCK
    n_blocks = n_slots // MOE_BLOCK
    d = h.shape[1]
    d_ff = w1.shape[2]
    tf = d_ff // nf
    assert tf * nf == d_ff and tf % LANES == 0 and MOE_BLOCK % nf == 0
    grid_spec = pltpu.PrefetchScalarGridSpec(
        num_scalar_prefetch=4,
        grid=(n_blocks, nf),
        in_specs=[pl.BlockSpec(memory_space=pl.ANY),
                  pl.BlockSpec((MOE_BLOCK, 1), lambda i, f, be, *_: (i, 0)),
                  pl.BlockSpec((None, d, tf), lambda i, f, be, *_: (be[i], 0, f)),
                  pl.BlockSpec((None, d, tf), lambda i, f, be, *_: (be[i], 0, f)),
                  pl.BlockSpec((None, tf, d), lambda i, f, be, *_: (be[i], f, 0))],
        out_specs=pl.BlockSpec(memory_space=pl.ANY),
        scratch_shapes=[pltpu.VMEM((2, MOE_BLOCK, d), F32),
                        pltpu.VMEM((MOE_BLOCK, d), BF16),
                        pltpu.VMEM((MOE_BLOCK, d), F32),
                        pltpu.VMEM((2, MOE_BLOCK, d), F32),
                        pltpu.VMEM((8, d), F32),
                        pltpu.SemaphoreType.DMA((2,)),
                        pltpu.SemaphoreType.DMA((2,))],
    )
    return pl.pallas_call(
        functools.partial(_expert_kernel, n_spare=n_spare),
        grid_spec=grid_spec,
        out_shape=jax.ShapeDtypeStruct((n_out, d), F32),
        compiler_params=_params(("arbitrary", "arbitrary")),
        name="moe_experts",
    )(block_e, slot_tok, dst_all, meta, h, slot_w, w1, w3, w2)


def _combine_kernel(x_ref, y0_ref, y1_ref, o_ref):
    o_ref[...] = x_ref[...] + (y0_ref[...] + y1_ref[...])


def _combine_call(x, y, *, tm):
    n, d = x.shape
    steps = n // tm
    return pl.pallas_call(
        _combine_kernel,
        grid=(steps,),
        in_specs=[pl.BlockSpec((tm, d), lambda i: (i, 0)),
                  pl.BlockSpec((tm, d), lambda i: (i, 0)),
                  pl.BlockSpec((tm, d), lambda i: (i + steps, 0))],
        out_specs=pl.BlockSpec((tm, d), lambda i: (i, 0)),
        out_shape=jax.ShapeDtypeStruct((n, d), F32),
        compiler_params=_params(("parallel",)),
        name="moe_combine",
    )(x, y, y)


def _route(logits):
    n_tok = logits.shape[0]
    n_assign = n_tok * TOP_K
    n_slots = -(-n_assign // MOE_BLOCK) * MOE_BLOCK + N_EXPERTS * MOE_BLOCK
    n_blocks = n_slots // MOE_BLOCK
    top_logit, top_idx = lax.top_k(logits, TOP_K)
    top_w = jax.nn.softmax(top_logit, axis=-1)
    onehot = jnp.sum(jax.nn.one_hot(top_idx, N_EXPERTS, dtype=jnp.int32), axis=1)
    csum = jnp.cumsum(onehot, axis=0)
    counts = csum[-1]
    padded = (counts + MOE_BLOCK - 1) // MOE_BLOCK * MOE_BLOCK
    padded_end = jnp.cumsum(padded)
    padded_start = padded_end - padded
    rank = jnp.take_along_axis(csum - onehot, top_idx, axis=1)
    slots = (padded_start[top_idx] + rank).astype(jnp.int32)
    dst = (jnp.arange(TOP_K, dtype=jnp.int32)[None, :] * n_tok
           + jnp.arange(n_tok, dtype=jnp.int32)[:, None])
    slot_dst = jnp.full((n_slots,), -1, jnp.int32).at[slots.reshape(-1)].set(dst.reshape(-1))
    real = slot_dst >= 0
    n_spare = n_slots - n_assign
    pad_rank = jnp.cumsum(jnp.logical_not(real).astype(jnp.int32)) - 1
    slot_tok = jnp.where(real, slot_dst % n_tok, 0)
    w_kmajor = top_w.T.reshape(-1)
    slot_w = jnp.where(real, w_kmajor[jnp.maximum(slot_dst, 0)], 0.0)
    slot_dst = jnp.where(real, slot_dst, n_assign + pad_rank)
    virtual = n_assign + n_spare + jnp.arange(MOE_BLOCK, dtype=jnp.int32)
    dst_all = jnp.concatenate([virtual, slot_dst])
    block_e = jnp.minimum(
        jnp.searchsorted(padded_end, jnp.arange(n_blocks) * MOE_BLOCK, side='right'),
        N_EXPERTS - 1).astype(jnp.int32)
    meta = jnp.stack([padded_end[-1] // MOE_BLOCK, padded_end[-1] - n_assign]).astype(jnp.int32)
    return block_e, slot_tok, dst_all, meta, slot_w.reshape(n_slots, 1), n_spare


def kernel(x, norm_mix_g, w_in, diff_q_norm_g, diff_k_norm_g, diff_lambda, diff_subln_g,
           w_gate, b_gate, w_branch_sb, w_branch_diff, w_out, norm_ffn_g,
           ffn_w_gate_up, ffn_w_down, moe_w_router, moe_w1, moe_w3, moe_w2):
    batch, seq, d = x.shape
    n = batch * seq
    depth = w_in.shape[0]
    tm = min(512, n)
    tq = min(256, seq)
    xf = x.reshape(n, d).astype(F32)
    bf = lambda w: w.astype(BF16)
    for i in range(depth):
        sbk, dfk, sbq, sbv, dfq, dfv = _proj_call(
            xf, norm_mix_g[i], w_in[i], diff_q_norm_g[i], diff_k_norm_g[i], tm=tm)
        y_sb = _sb_call(sbq, sbk, sbv, batch=batch, tq=tq)
        lambda_init = 0.8 - 0.6 * math.exp(-0.3 * i)
        y_df = _diff_call(dfq, dfk, dfv, diff_lambda[i], diff_subln_g[i],
                          batch=batch, tq=tq, tk=min(1024, seq), lambda_init=lambda_init)
        j = i // 2
        dense = i % 2 == 0
        outs = _merge_call(xf, y_sb, y_df, norm_mix_g[i], bf(w_gate[i]), b_gate[i],
                           bf(w_branch_sb[i]), bf(w_branch_diff[i]), bf(w_out[i]),
                           norm_ffn_g[i], None if dense else moe_w_router[j], tm=tm)
        if dense:
            xf, h2 = outs
            xf = _swiglu_call(xf, h2, bf(ffn_w_gate_up[j]), bf(ffn_w_down[j]), tm=tm, nf=2)
        else:
            xf, h2, logits = outs
            block_e, slot_tok, dst_all, meta, slot_w, n_spare = _route(logits[:, :N_EXPERTS])
            y = _expert_call(block_e, slot_tok, dst_all, meta, h2, slot_w,
                             bf(moe_w1[j]), bf(moe_w3[j]), bf(moe_w2[j]), n_spare=n_spare, nf=4)
            xf = _combine_call(xf, y, tm=tm)
    return xf.reshape(batch, seq, d).astype(x.dtype)
```

```python
import functools
import math

import jax
import jax.numpy as jnp
from jax import lax
from jax.experimental import pallas as pl
from jax.experimental.pallas import tpu as pltpu

F32 = jnp.float32
BF16 = jnp.bfloat16

HEAD_DIM = 64
NORM_EPS = 1e-6
N_EXPERTS = 8
TOP_K = 2
MOE_BLOCK = 512
LANES = 128
VMEM_LIMIT = 56 * 1024 * 1024

EXP_UNDERFLOW = -104.0
EXP2_UNDERFLOW = 152.0
MASKED = -1e30
LOG2E = math.log2(math.e)
V_ROWS = LANES + 16


def _dot(a, b):
    return jnp.dot(a, b, preferred_element_type=F32)


def _dot_nt(a, b):
    return lax.dot_general(a, b, (((1,), (1,)), ((), ())), preferred_element_type=F32)


def _rms(x, g):
    ms = jnp.mean(x * x, axis=-1, keepdims=True)
    return (x * lax.rsqrt(ms + NORM_EPS)) * g


def _split(x):
    hi = x.astype(BF16)
    return hi, (x - hi.astype(F32)).astype(BF16)


def _params(sem, vmem=VMEM_LIMIT):
    return pltpu.CompilerParams(dimension_semantics=sem, vmem_limit_bytes=vmem)


def _proj_kernel(x_ref, g_ref, wk_ref, wt_ref, bd_ref, gq_ref, gk_ref,
                 sbk_ref, dfk_ref, sbq_ref, sbv_ref, dfq_ref, dfv_ref, *, width):
    scale = HEAD_DIM ** -0.5
    h = _rms(x_ref[...], g_ref[...]).astype(BF16)
    bd = bd_ref[...]

    def feature_major(j):
        return _dot_nt(wt_ref[j * width:(j + 1) * width, :], h)

    sbk_ref[...] = _dot(h, wk_ref[:, :width]).astype(BF16)
    k = _dot(h, wk_ref[:, width:])
    hi, lo = _split(k * k)
    ss = _dot(hi, bd) + _dot(lo, bd)
    dfk_ref[...] = ((k * lax.rsqrt(ss * (1.0 / HEAD_DIM) + NORM_EPS)) * gk_ref[...]).astype(BF16)

    sbq_ref[...] = (feature_major(0) * scale).astype(BF16)
    sbv_ref[...] = feature_major(1).astype(BF16)
    q = feature_major(2)
    hi, lo = _split(q * q)
    ss = _dot(bd, hi) + _dot(bd, lo)
    q = (q * lax.rsqrt(ss * (1.0 / HEAD_DIM) + NORM_EPS)) * gq_ref[...]
    dfq_ref[...] = (q * (scale * LOG2E)).astype(BF16)
    v = feature_major(3).astype(BF16)
    for hd in range(width // LANES):
        dfv_ref[hd * V_ROWS:hd * V_ROWS + LANES, :] = v[hd * LANES:(hd + 1) * LANES, :]
        dfv_ref[hd * V_ROWS + LANES:(hd + 1) * V_ROWS, :] = jnp.ones(
            (V_ROWS - LANES, v.shape[1]), BF16)


def _proj_call(x, g, w_in, gq, gk, *, tm):
    n, d = x.shape
    width = w_in.shape[1] // 6
    cols = lambda j: w_in[:, j * width:(j + 1) * width]
    wk = jnp.concatenate([cols(1), cols(4)], axis=1).astype(BF16)
    wt = jnp.concatenate([cols(0), cols(2), cols(3), cols(5)], axis=1).T.astype(BF16)
    grp = lax.broadcasted_iota(jnp.int32, (width, width), 0) // HEAD_DIM
    bd = (grp == grp.T).astype(BF16)
    const = lambda a: pl.BlockSpec(a.shape, lambda i: (0, 0))
    tok_major = pl.BlockSpec((tm, width), lambda i: (i, 0))
    feat_major = pl.BlockSpec((width, tm), lambda i: (0, i))
    v_rows = width // LANES * V_ROWS
    args = [x, g.reshape(1, d), wk, wt, bd,
            jnp.tile(gq, width // HEAD_DIM).reshape(width, 1),
            jnp.tile(gk, width // HEAD_DIM).reshape(1, width)]
    return pl.pallas_call(
        functools.partial(_proj_kernel, width=width),
        grid=(n // tm,),
        in_specs=[pl.BlockSpec((tm, d), lambda i: (i, 0))] + [const(a) for a in args[1:]],
        out_specs=[tok_major] * 2 + [feat_major] * 3 + [pl.BlockSpec((v_rows, tm), lambda i: (0, i))],
        out_shape=[jax.ShapeDtypeStruct((n, width), BF16)] * 2
        + [jax.ShapeDtypeStruct((width, n), BF16)] * 3 + [jax.ShapeDtypeStruct((v_rows, n), BF16)],
        compiler_params=_params(("parallel",)),
        name="proj",
    )(*args)


def _attn_specs(batch, seq, tq):
    nq = seq // tq
    q_spec = pl.BlockSpec((LANES, tq), lambda b, h, i: (h, b * nq + i))
    k_spec = pl.BlockSpec((seq, LANES), lambda b, h, i: (b, h))
    v_spec = pl.BlockSpec((LANES, seq), lambda b, h, i: (h, b))
    o_spec = pl.BlockSpec((tq, LANES), lambda b, h, i: (b * nq + i, h))
    return q_spec, k_spec, v_spec, o_spec


def _sb_kernel(q_ref, k_ref, v_ref, o_ref, acc_ref, l0_ref, l1_ref, row0_ref, row1_ref,
               lb0_ref, lb1_ref, hi0_ref, hi1_ref, lo0_ref, lo1_ref, mask_ref, later_ref, *, tq):
    qi = pl.program_id(2)
    n_heads = LANES // HEAD_DIM
    l_refs, lb_refs, row_refs = (l0_ref, l1_ref), (lb0_ref, lb1_ref), (row0_ref, row1_ref)
    hi_refs, lo_refs = (hi0_ref, hi1_ref), (lo0_ref, lo1_ref)

    @pl.when(qi == 0)
    def _():
        key = lax.broadcasted_iota(jnp.int32, (tq, tq), 0)
        other = lax.broadcasted_iota(jnp.int32, (tq, tq), 1)
        mask_ref[0] = jnp.where(key < other, 1.0, 0.0).astype(F32)
        mask_ref[1] = jnp.ones((tq, tq), F32)
        later_ref[...] = jnp.where(other > key, 1.0, 0.0).astype(BF16)

    feat = lax.broadcasted_iota(jnp.int32, (LANES, tq), 0)
    q = q_ref[...]
    q_heads = [jnp.where((feat >= hh * HEAD_DIM) & (feat < (hh + 1) * HEAD_DIM), q,
                         jnp.zeros_like(q)) for hh in range(n_heads)]

    acc_ref[...] = jnp.zeros_like(acc_ref)
    for hh in range(n_heads):
        l_refs[hh][...] = jnp.zeros_like(l_refs[hh])

    def logits(hh, kb):
        start = pl.multiple_of(kb * tq, tq)
        z = _dot(k_ref[pl.ds(start, tq), :], q_heads[hh])
        sp = jnp.maximum(z, 0.0) + jnp.log(1.0 + jnp.exp(-jnp.abs(z)))
        log_keep = -sp * mask_ref[jnp.where(kb == qi, 0, 1)]
        hi, lo = _split(log_keep)
        hi_refs[hh][...] = hi
        lo_refs[hh][...] = lo
        lb_refs[hh][...] = z - sp
        row_refs[hh][...] = log_keep[:1, :]

    def weights(hh, kb):
        feats = slice(hh * HEAD_DIM, (hh + 1) * HEAD_DIM)
        start = pl.multiple_of(kb * tq, tq)
        later_keys = later_ref[...]
        later = _dot(later_keys, hi_refs[hh][...]) + _dot(later_keys, lo_refs[hh][...])
        carry_l = l_refs[hh][...]
        w = jnp.exp(lb_refs[hh][...] + later + carry_l) * mask_ref[jnp.where(kb == qi, 0, 1)]
        acc_ref[feats, :] += _dot(v_ref[feats, pl.ds(start, tq)], w.astype(BF16))
        l_new = carry_l + later[:1, :] + row_refs[hh][...]
        l_refs[hh][...] = l_new
        return jnp.max(l_new)

    logits(0, qi)

    def cond(c):
        it, lm = c
        return jnp.logical_and(it <= qi, lm > EXP_UNDERFLOW)

    def body(c):
        it, _ = c
        kb = qi - it
        logits(1, kb)
        lm0 = weights(0, kb)
        logits(0, jnp.maximum(kb - 1, 0))
        lm1 = weights(1, kb)
        return it + 1, jnp.maximum(lm0, lm1)

    lax.while_loop(cond, body, (jnp.int32(0), jnp.float32(0.0)))
    o_ref[...] = acc_ref[...].T.astype(o_ref.dtype)


def _sb_call(q, k, v, *, batch, tq):
    n, width = k.shape
    q_spec, k_spec, v_spec, o_spec = _attn_specs(batch, n // batch, tq)
    return pl.pallas_call(
        functools.partial(_sb_kernel, tq=tq),
        grid=(batch, width // LANES, n // batch // tq),
        in_specs=[q_spec, k_spec, v_spec],
        out_specs=o_spec,
        out_shape=jax.ShapeDtypeStruct((n, width), BF16),
        scratch_shapes=[pltpu.VMEM((LANES, tq), F32)] + [pltpu.VMEM((1, tq), F32)] * 4
        + [pltpu.VMEM((tq, tq), F32)] * 2 + [pltpu.VMEM((tq, tq), BF16)] * 4
        + [pltpu.VMEM((2, tq, tq), F32), pltpu.VMEM((tq, tq), BF16)],
        compiler_params=_params(("parallel", "parallel", "arbitrary")),
        name="sb_attn",
    )(q, k, v)


def _diff_kernel(slopes_ref, lam_ref, g_ref, q_ref, k_ref, v_ref, o_ref, bias_ref,
                 s0_ref, s1_ref, mb0_ref, mb1_ref, m0_ref, m1_ref, acc0_ref, acc1_ref, kmax_ref,
                 *, tq, tk, lambda_init):
    h = pl.program_id(1)
    qi = pl.program_id(2)
    slope = slopes_ref[0, h]
    inv_slope = slopes_ref[1, h]
    nsub = tk // tq
    s_refs, mb_refs = (s0_ref, s1_ref), (mb0_ref, mb1_ref)
    m_refs, acc_refs = (m0_ref, m1_ref), (acc0_ref, acc1_ref)

    @pl.when(qi == 0)
    def _():
        key = lax.broadcasted_iota(jnp.int32, (tk, tq), 0)
        query = lax.broadcasted_iota(jnp.int32, (tk, tq), 1)
        key_ahead = key - query
        rel_bias = slope * key_ahead.astype(F32)
        bias_ref[nsub] = rel_bias
        for r in range(nsub):
            bias_ref[r] = jnp.where(key_ahead <= r * tq, rel_bias, MASKED)
        kf = k_ref[...].astype(F32)
        kmax_ref[0] = jnp.max(jnp.sum(kf * kf, axis=1, keepdims=True))

    feat = lax.broadcasted_iota(jnp.int32, (LANES, tq), 0)
    q = q_ref[...]
    q_maps = [jnp.where((feat >= c * HEAD_DIM) & (feat < (c + 1) * HEAD_DIM), q, jnp.zeros_like(q))
              for c in range(2)]

    for c in range(2):
        m_refs[c][...] = jnp.full_like(m_refs[c], MASKED)
        acc_refs[c][...] = jnp.zeros_like(acc_refs[c])

    last = (qi * tq) // tk

    qf = q.astype(F32)
    qk = jnp.max(jnp.sqrt(jnp.sum(qf * qf, axis=0, keepdims=True) * kmax_ref[0]))
    skip_dist = (EXP2_UNDERFLOW + 2.01 * qk) * inv_slope
    first = jnp.clip(((qi * tq).astype(F32) - skip_dist) * (1.0 / tk), 0.0, last.astype(F32))
    first = first.astype(jnp.int32)

    def offset(kb):
        return -slope * (qi * tq - kb * tk).astype(F32)

    half = tk // 2

    def scores(c, kb):
        sel = jnp.where(kb == last, qi % nsub, nsub)
        mb = None
        for part in range(2):
            rows = pl.ds(part * half, half)
            start = pl.multiple_of(kb * tk + part * half, half)
            s = _dot(k_ref[pl.ds(start, half), :], q_maps[c]) + bias_ref[sel, rows, :]
            s_refs[c][rows, :] = s
            part_max = jnp.max(s, axis=0, keepdims=True)
            mb = part_max if mb is None else jnp.maximum(mb, part_max)
        mb_refs[c][...] = mb + offset(kb)

    def accumulate(c, kb):
        m_old = m_refs[c][...]
        m_new = jnp.maximum(m_old, mb_refs[c][...])
        alpha = jnp.exp2(m_old - m_new)
        shift = m_new - offset(kb)
        pv = None
        for part in range(2):
            start = pl.multiple_of(kb * tk + part * half, half)
            p = jnp.exp2(s_refs[c][pl.ds(part * half, half), :] - shift)
            part_pv = _dot(v_ref[:, pl.ds(start, half)], p.astype(BF16))
            pv = part_pv if pv is None else pv + part_pv
        acc_refs[c][...] = alpha * acc_refs[c][...] + pv
        m_refs[c][...] = m_new

    scores(0, first)

    def step(kb):
        scores(1, kb)
        accumulate(0, kb)
        scores(0, jnp.minimum(kb + 1, last))
        accumulate(1, kb)

    def pair(j, carry):
        step(first + 2 * j)
        step(first + 2 * j + 1)
        return carry

    n_blocks = last + 1 - first
    lax.fori_loop(0, n_blocks // 2, pair, 0)

    @pl.when(n_blocks % 2 == 1)
    def _():
        step(last)

    lp = lam_ref[...]
    lam = (jnp.exp(jnp.sum(lp[0:1] * lp[1:2], axis=-1, keepdims=True))
           - jnp.exp(jnp.sum(lp[2:3] * lp[3:4], axis=-1, keepdims=True)) + lambda_init)
    a0, a1 = acc0_ref[...], acc1_ref[...]
    o = (a0[:LANES] * (1.0 / a0[LANES:LANES + 1])
         - lam * (a1[:LANES] * (1.0 / a1[LANES:LANES + 1])))
    ms = jnp.mean(o * o, axis=0, keepdims=True)
    o = ((o * lax.rsqrt(ms + NORM_EPS)) * g_ref[...]) * (1.0 - lambda_init)
    o_ref[...] = o.T.astype(o_ref.dtype)


def _diff_call(q, k, v, lam_params, subln_g, *, batch, tq, tk, lambda_init):
    n, width = k.shape
    heads = width // LANES
    assert tk % tq == 0
    slopes = [2.0 ** (-8.0 * (h + 1) / heads) * LOG2E for h in range(heads)]
    slopes = jnp.asarray([slopes, [1.0 / s for s in slopes]], dtype=F32)
    seq = n // batch
    q_spec, k_spec, _, o_spec = _attn_specs(batch, seq, tq)
    v_spec = pl.BlockSpec((V_ROWS, seq), lambda b, h, i: (h, b))
    return pl.pallas_call(
        functools.partial(_diff_kernel, tq=tq, tk=tk, lambda_init=lambda_init),
        grid=(batch, heads, seq // tq),
        in_specs=[pl.BlockSpec(memory_space=pltpu.SMEM),
                  pl.BlockSpec(lam_params.shape, lambda b, h, i: (0, 0)),
                  pl.BlockSpec((LANES, 1), lambda b, h, i: (0, 0)),
                  q_spec, k_spec, v_spec],
        out_specs=o_spec,
        out_shape=jax.ShapeDtypeStruct((n, width), BF16),
        scratch_shapes=[pltpu.VMEM((tk // tq + 1, tk, tq), F32)]
        + [pltpu.VMEM((tk, tq), F32)] * 2 + [pltpu.VMEM((1, tq), F32)] * 4
        + [pltpu.VMEM((V_ROWS, tq), F32)] * 2 + [pltpu.SMEM((1,), F32)],
        compiler_params=_params(("parallel", "parallel", "arbitrary")),
        name="diff_attn",
    )(slopes, lam_params.astype(F32), subln_g.reshape(LANES, 1).astype(F32), q, k, v)


def _merge_kernel(*refs, router):
    (x_ref, ysb_ref, ydf_ref, gmix_ref, wg_ref, bg_ref, wsb_ref, wdf_ref, wo_ref,
     gffn_ref) = refs[:10]
    rest = refs[10:]
    x = x_ref[...]
    d = x.shape[-1]
    h = _rms(x, gmix_ref[...]).astype(BF16)
    gates = jax.nn.sigmoid(_dot(h, wg_ref[...]) + bg_ref[...])
    merged = (gates[:, :d] * _dot(ysb_ref[...], wsb_ref[...])
              + gates[:, d:] * _dot(ydf_ref[...], wdf_ref[...]))
    xn = x + _dot(merged.astype(BF16), wo_ref[...])
    h2 = _rms(xn, gffn_ref[...])
    if router:
        wr_ref, xo_ref, h2_ref, lg_ref = rest
        lg_ref[...] = _dot(h2.astype(BF16), wr_ref[...])
    else:
        xo_ref, h2_ref = rest
    xo_ref[...] = xn
    h2_ref[...] = h2.astype(h2_ref.dtype)


def _merge_call(x, ysb, ydf, gmix, wg, bg, wsb, wdf, wo, gffn, w_router, *, tm):
    n, d = x.shape
    router = w_router is not None
    row = lambda w: pl.BlockSpec((tm, w), lambda i: (i, 0))
    const = lambda a: pl.BlockSpec(a.shape, lambda i: (0, 0))
    args = [x, ysb, ydf, gmix.reshape(1, d), wg, bg.reshape(1, -1), wsb, wdf, wo,
            gffn.reshape(1, d)]
    in_specs = [row(d), row(ysb.shape[1]), row(ydf.shape[1])] + [const(a) for a in args[3:]]
    out_specs = [row(d), row(d)]
    out_shape = [jax.ShapeDtypeStruct((n, d), F32),
                 jax.ShapeDtypeStruct((n, d), F32 if router else BF16)]
    if router:
        wr = jnp.zeros((d, LANES), BF16).at[:, :w_router.shape[1]].set(w_router.astype(BF16))
        args.append(wr)
        in_specs.append(const(wr))
        out_specs.append(row(LANES))
        out_shape.append(jax.ShapeDtypeStruct((n, LANES), F32))
    return pl.pallas_call(
        functools.partial(_merge_kernel, router=router),
        grid=(n // tm,),
        in_specs=in_specs,
        out_specs=out_specs,
        out_shape=out_shape,
        compiler_params=_params(("parallel",)),
        name="merge_router" if router else "merge",
    )(*args)


def _swiglu_kernel(x_ref, h_ref, wg_ref, wu_ref, wd_ref, o_ref, acc_ref):
    f = pl.program_id(1)

    @pl.when(f == 0)
    def _():
        acc_ref[...] = jnp.zeros_like(acc_ref)

    h = h_ref[...]
    g = _dot(h, wg_ref[...])
    u = _dot(h, wu_ref[...])
    act = (g * jax.nn.sigmoid(g)) * u
    acc_ref[...] += _dot(act.astype(BF16), wd_ref[...])

    @pl.when(f == pl.num_programs(1) - 1)
    def _():
        o_ref[...] = x_ref[...] + acc_ref[...]


def _swiglu_call(x, h, w_gate_up, w_down, *, tm, nf):
    n, d = x.shape
    d_ff = w_down.shape[0]
    tf = d_ff // nf
    assert tf * nf == d_ff and tf % LANES == 0
    row = pl.BlockSpec((tm, d), lambda i, f: (i, 0))
    return pl.pallas_call(
        _swiglu_kernel,
        grid=(n // tm, nf),
        in_specs=[row, row,
                  pl.BlockSpec((d, tf), lambda i, f: (0, f)),
                  pl.BlockSpec((d, tf), lambda i, f: (0, f + nf)),
                  pl.BlockSpec((tf, d), lambda i, f: (f, 0))],
        out_specs=row,
        out_shape=jax.ShapeDtypeStruct((n, d), F32),
        scratch_shapes=[pltpu.VMEM((tm, d), F32)],
        compiler_params=_params(("parallel", "arbitrary")),
        name="swiglu",
    )(x, h, w_gate_up, w_gate_up, w_down)


def _expert_kernel(be_ref, tok_ref, dst_ref, meta_ref, h_hbm, sw_ref, w1_ref, w3_ref, w2_ref,
                   y_hbm, xbuf, xb_ref, acc_ref, ybuf, zrow_ref, gsem, ssem, *, n_spare):
    del be_ref
    i = pl.program_id(0)
    f = pl.program_id(1)
    nf = pl.num_programs(1)
    nused = meta_ref[0]
    rows = xb_ref.shape[0]

    def gather_copy(blk, r):
        return pltpu.make_async_copy(h_hbm.at[pl.ds(tok_ref[blk * rows + r], 1)],
                                     xbuf.at[blk % 2, pl.ds(r, 1)], gsem.at[blk % 2])

    def scatter_copy(blk, r):
        return pltpu.make_async_copy(ybuf.at[(blk + 2) % 2, pl.ds(r, 1)],
                                     y_hbm.at[pl.ds(dst_ref[(blk + 1) * rows + r], 1)],
                                     ssem.at[(blk + 2) % 2])

    def for_rows(lo, n, fn):
        def body(r, carry):
            fn(lo + r)
            return carry
        lax.fori_loop(0, n, body, 0, unroll=8)

    @pl.when(jnp.logical_and(i == 0, f == 0))
    def _():
        ybuf[1] = jnp.zeros(ybuf.shape[1:], ybuf.dtype)
        for_rows(0, rows, lambda r: gather_copy(0, r).start())

    @pl.when(jnp.logical_and(f == 0, i <= nused))
    def _():
        for_rows(0, rows, lambda r: gather_copy(i, r).wait())

    @pl.when(jnp.logical_and(f == 0, i < nused))
    def _():
        xb_ref[...] = xbuf[i % 2].astype(BF16)
        acc_ref[...] = jnp.zeros_like(acc_ref)

    chunk = rows // nf

    @pl.when(i < nused)
    def _():
        for r in range(chunk):
            gather_copy(i + 1, f * chunk + r).start()
            scatter_copy(i - 1, f * chunk + r).start()
        xb = xb_ref[...]
        g = _dot(xb, w1_ref[...])
        u = _dot(xb, w3_ref[...])
        hid = (g * jax.nn.sigmoid(g)) * u
        acc_ref[...] += _dot(hid.astype(BF16), w2_ref[...])

    @pl.when(i == nused)
    def _():
        for_rows(f * chunk, chunk, lambda r: scatter_copy(i - 1, r).start())

    @pl.when(f == nf - 1)
    def _():
        @pl.when(jnp.logical_and(i < nused, i >= 1))
        def _():
            for_rows(0, rows, lambda r: scatter_copy(i - 2, r).wait())

        @pl.when(i < nused)
        def _():
            ybuf[i % 2] = acc_ref[...] * sw_ref[...]

        @pl.when(i == nused)
        def _():
            @pl.when(i >= 1)
            def _():
                for_rows(0, rows, lambda r: scatter_copy(i - 2, r).wait())
            for_rows(0, rows, lambda r: scatter_copy(i - 1, r).wait())

            zrow_ref[...] = jnp.zeros_like(zrow_ref)
            n_pad = meta_ref[1]
            spare_lo = y_hbm.shape[0] - rows - n_spare

            def zero_copy(r):
                return pltpu.make_async_copy(zrow_ref.at[pl.ds(0, 1)],
                                             y_hbm.at[pl.ds(spare_lo + r, 1)], ssem.at[0])

            def fill(r, carry):
                zero_copy(r).start()
                return carry

            def drain(r, carry):
                zero_copy(r).wait()
                return carry

            lax.fori_loop(n_pad, n_spare, fill, 0)
            lax.fori_loop(n_pad, n_spare, drain, 0)


def _expert_call(block_e, slot_tok, dst_all, meta, h, slot_w, w1, w3, w2, *, n_spare, nf):
    n_slots = slot_tok.shape[0]
    n_out = h.shape[0] * TOP_K + n_spare + MOE_BLOCK
    n_blocks = n_slots // MOE_BLOCK
    d = h.shape[1]
    d_ff = w1.shape[2]
    tf = d_ff // nf
    assert tf * nf == d_ff and tf % LANES == 0 and MOE_BLOCK % nf == 0
    grid_spec = pltpu.PrefetchScalarGridSpec(
        num_scalar_prefetch=4,
        grid=(n_blocks, nf),
        in_specs=[pl.BlockSpec(memory_space=pl.ANY),
                  pl.BlockSpec((MOE_BLOCK, 1), lambda i, f, be, *_: (i, 0)),
                  pl.BlockSpec((None, d, tf), lambda i, f, be, *_: (be[i], 0, f)),
                  pl.BlockSpec((None, d, tf), lambda i, f, be, *_: (be[i], 0, f)),
                  pl.BlockSpec((None, tf, d), lambda i, f, be, *_: (be[i], f, 0))],
        out_specs=pl.BlockSpec(memory_space=pl.ANY),
        scratch_shapes=[pltpu.VMEM((2, MOE_BLOCK, d), F32),
                        pltpu.VMEM((MOE_BLOCK, d), BF16),
                        pltpu.VMEM((MOE_BLOCK, d), F32),
                        pltpu.VMEM((2, MOE_BLOCK, d), F32),
                        pltpu.VMEM((8, d), F32),
                        pltpu.SemaphoreType.DMA((2,)),
                        pltpu.SemaphoreType.DMA((2,))],
    )
    return pl.pallas_call(
        functools.partial(_expert_kernel, n_spare=n_spare),
        grid_spec=grid_spec,
        out_shape=jax.ShapeDtypeStruct((n_out, d), F32),
        compiler_params=_params(("arbitrary", "arbitrary")),
        name="moe_experts",
    )(block_e, slot_tok, dst_all, meta, h, slot_w, w1, w3, w2)


def _combine_kernel(x_ref, y0_ref, y1_ref, o_ref):
    o_ref[...] = x_ref[...] + (y0_ref[...] + y1_ref[...])


def _combine_call(x, y, *, tm):
    n, d = x.shape
    steps = n // tm
    return pl.pallas_call(
        _combine_kernel,
        grid=(steps,),
        in_specs=[pl.BlockSpec((tm, d), lambda i: (i, 0)),
                  pl.BlockSpec((tm, d), lambda i: (i, 0)),
                  pl.BlockSpec((tm, d), lambda i: (i + steps, 0))],
        out_specs=pl.BlockSpec((tm, d), lambda i: (i, 0)),
        out_shape=jax.ShapeDtypeStruct((n, d), F32),
        compiler_params=_params(("parallel",)),
        name="moe_combine",
    )(x, y, y)


def _route(logits):
    n_tok = logits.shape[0]
    n_assign = n_tok * TOP_K
    n_slots = -(-n_assign // MOE_BLOCK) * MOE_BLOCK + N_EXPERTS * MOE_BLOCK
    n_blocks = n_slots // MOE_BLOCK
    top_logit, top_idx = lax.top_k(logits, TOP_K)
    top_w = jax.nn.softmax(top_logit, axis=-1)
    onehot = jnp.sum(jax.nn.one_hot(top_idx, N_EXPERTS, dtype=jnp.int32), axis=1)
    csum = jnp.cumsum(onehot, axis=0)
    counts = csum[-1]
    padded = (counts + MOE_BLOCK - 1) // MOE_BLOCK * MOE_BLOCK
    padded_end = jnp.cumsum(padded)
    padded_start = padded_end - padded
    rank = jnp.take_along_axis(csum - onehot, top_idx, axis=1)
    slots = (padded_start[top_idx] + rank).astype(jnp.int32)
    dst = (jnp.arange(TOP_K, dtype=jnp.int32)[None, :] * n_tok
           + jnp.arange(n_tok, dtype=jnp.int32)[:, None])
    slot_dst = jnp.full((n_slots,), -1, jnp.int32).at[slots.reshape(-1)].set(dst.reshape(-1))
    real = slot_dst >= 0
    n_spare = n_slots - n_assign
    pad_rank = jnp.cumsum(jnp.logical_not(real).astype(jnp.int32)) - 1
    slot_tok = jnp.where(real, slot_dst % n_tok, 0)
    w_kmajor = top_w.T.reshape(-1)
    slot_w = jnp.where(real, w_kmajor[jnp.maximum(slot_dst, 0)], 0.0)
    slot_dst = jnp.where(real, slot_dst, n_assign + pad_rank)
    virtual = n_assign + n_spare + jnp.arange(MOE_BLOCK, dtype=jnp.int32)
    dst_all = jnp.concatenate([virtual, slot_dst])
    block_e = jnp.minimum(
        jnp.searchsorted(padded_end, jnp.arange(n_blocks) * MOE_BLOCK, side='right'),
        N_EXPERTS - 1).astype(jnp.int32)
    meta = jnp.stack([padded_end[-1] // MOE_BLOCK, padded_end[-1] - n_assign]).astype(jnp.int32)
    return block_e, slot_tok, dst_all, meta, slot_w.reshape(n_slots, 1), n_spare


def kernel(x, norm_mix_g, w_in, diff_q_norm_g, diff_k_norm_g, diff_lambda, diff_subln_g,
           w_gate, b_gate, w_branch_sb, w_branch_diff, w_out, norm_ffn_g,
           ffn_w_gate_up, ffn_w_down, moe_w_router, moe_w1, moe_w3, moe_w2):
    batch, seq, d = x.shape
    n = batch * seq
    depth = w_in.shape[0]
    tm = min(512, n)
    tq = min(256, seq)
    xf = x.reshape(n, d).astype(F32)
    bf = lambda w: w.astype(BF16)
    for i in range(depth):
        sbk, dfk, sbq, sbv, dfq, dfv = _proj_call(
            xf, norm_mix_g[i], w_in[i], diff_q_norm_g[i], diff_k_norm_g[i], tm=tm)
        y_sb = _sb_call(sbq, sbk, sbv, batch=batch, tq=tq)
        lambda_init = 0.8 - 0.6 * math.exp(-0.3 * i)
        y_df = _diff_call(dfq, dfk, dfv, diff_lambda[i], diff_subln_g[i],
                          batch=batch, tq=tq, tk=min(1024, seq), lambda_init=lambda_init)
        j = i // 2
        dense = i % 2 == 0
        outs = _merge_call(xf, y_sb, y_df, norm_mix_g[i], bf(w_gate[i]), b_gate[i],
                           bf(w_branch_sb[i]), bf(w_branch_diff[i]), bf(w_out[i]),
                           norm_ffn_g[i], None if dense else moe_w_router[j], tm=tm)
        if dense:
            xf, h2 = outs
            xf = _swiglu_call(xf, h2, bf(ffn_w_gate_up[j]), bf(ffn_w_down[j]), tm=tm, nf=2)
        else:
            xf, h2, logits = outs
            block_e, slot_tok, dst_all, meta, slot_w, n_spare = _route(logits[:, :N_EXPERTS])
            y = _expert_call(block_e, slot_tok, dst_all, meta, h2, slot_w,
                             bf(moe_w1[j]), bf(moe_w3[j]), bf(moe_w2[j]), n_spare=n_spare, nf=4)
            xf = _combine_call(xf, y, tm=tm)
    return xf.reshape(batch, seq, d).astype(x.dtype)
```

```python
import functools
import math

import jax
import jax.numpy as jnp
from jax import lax
from jax.experimental import pallas as pl
from jax.experimental.pallas import tpu as pltpu

F32 = jnp.float32
BF16 = jnp.bfloat16

HEAD_DIM = 64
NORM_EPS = 1e-6
N_EXPERTS = 8
TOP_K = 2
MOE_BLOCK = 512
LANES = 128
VMEM_LIMIT = 56 * 1024 * 1024

EXP_UNDERFLOW = -104.0
EXP2_UNDERFLOW = 152.0
MASKED = -1e30
LOG2E = math.log2(math.e)
V_ROWS = LANES + 16


def _dot(a, b):
    return jnp.dot(a, b, preferred_element_type=F32)


def _dot_nt(a, b):
    return lax.dot_general(a, b, (((1,), (1,)), ((), ())), preferred_element_type=F32)


def _rms(x, g):
    ms = jnp.mean(x * x, axis=-1, keepdims=True)
    return (x * lax.rsqrt(ms + NORM_EPS)) * g


def _split(x):
    hi = x.astype(BF16)
    return hi, (x - hi.astype(F32)).astype(BF16)


def _params(sem, vmem=VMEM_LIMIT):
    return pltpu.CompilerParams(dimension_semantics=sem, vmem_limit_bytes=vmem)


def _proj_kernel(x_ref, g_ref, wk_ref, wt_ref, bd_ref, gq_ref, gk_ref,
                 sbk_ref, dfk_ref, sbq_ref, sbv_ref, dfq_ref, dfv_ref, *, width):
    scale = HEAD_DIM ** -0.5
    h = _rms(x_ref[...], g_ref[...]).astype(BF16)
    bd = bd_ref[...]

    def feature_major(j):
        return _dot_nt(wt_ref[j * width:(j + 1) * width, :], h)

    sbk_ref[...] = _dot(h, wk_ref[:, :width]).astype(BF16)
    k = _dot(h, wk_ref[:, width:])
    hi, lo = _split(k * k)
    ss = _dot(hi, bd) + _dot(lo, bd)
    dfk_ref[...] = ((k * lax.rsqrt(ss * (1.0 / HEAD_DIM) + NORM_EPS)) * gk_ref[...]).astype(BF16)

    sbq_ref[...] = (feature_major(0) * scale).astype(BF16)
    sbv_ref[...] = feature_major(1).astype(BF16)
    q = feature_major(2)
    hi, lo = _split(q * q)
    ss = _dot(bd, hi) + _dot(bd, lo)
    q = (q * lax.rsqrt(ss * (1.0 / HEAD_DIM) + NORM_EPS)) * gq_ref[...]
    dfq_ref[...] = (q * (scale * LOG2E)).astype(BF16)
    v = feature_major(3).astype(BF16)
    for hd in range(width // LANES):
        dfv_ref[hd * V_ROWS:hd * V_ROWS + LANES, :] = v[hd * LANES:(hd + 1) * LANES, :]
        dfv_ref[hd * V_ROWS + LANES:(hd + 1) * V_ROWS, :] = jnp.ones(
            (V_ROWS - LANES, v.shape[1]), BF16)


def _proj_call(x, g, w_in, gq, gk, *, tm):
    n, d = x.shape
    width = w_in.shape[1] // 6
    cols = lambda j: w_in[:, j * width:(j + 1) * width]
    wk = jnp.concatenate([cols(1), cols(4)], axis=1).astype(BF16)
    wt = jnp.concatenate([cols(0), cols(2), cols(3), cols(5)], axis=1).T.astype(BF16)
    grp = lax.broadcasted_iota(jnp.int32, (width, width), 0) // HEAD_DIM
    bd = (grp == grp.T).astype(BF16)
    const = lambda a: pl.BlockSpec(a.shape, lambda i: (0, 0))
    tok_major = pl.BlockSpec((tm, width), lambda i: (i, 0))
    feat_major = pl.BlockSpec((width, tm), lambda i: (0, i))
    v_rows = width // LANES * V_ROWS
    args = [x, g.reshape(1, d), wk, wt, bd,
            jnp.tile(gq, width // HEAD_DIM).reshape(width, 1),
            jnp.tile(gk, width // HEAD_DIM).reshape(1, width)]
    return pl.pallas_call(
        functools.partial(_proj_kernel, width=width),
        grid=(n // tm,),
        in_specs=[pl.BlockSpec((tm, d), lambda i: (i, 0))] + [const(a) for a in args[1:]],
        out_specs=[tok_major] * 2 + [feat_major] * 3 + [pl.BlockSpec((v_rows, tm), lambda i: (0, i))],
        out_shape=[jax.ShapeDtypeStruct((n, width), BF16)] * 2
        + [jax.ShapeDtypeStruct((width, n), BF16)] * 3 + [jax.ShapeDtypeStruct((v_rows, n), BF16)],
        compiler_params=_params(("parallel",)),
        name="proj",
    )(*args)


def _attn_specs(batch, seq, tq):
    nq = seq // tq
    q_spec = pl.BlockSpec((LANES, tq), lambda b, h, i: (h, b * nq + i))
    k_spec = pl.BlockSpec((seq, LANES), lambda b, h, i: (b, h))
    v_spec = pl.BlockSpec((LANES, seq), lambda b, h, i: (h, b))
    o_spec = pl.BlockSpec((tq, LANES), lambda b, h, i: (b * nq + i, h))
    return q_spec, k_spec, v_spec, o_spec


def _sb_kernel(q_ref, k_ref, v_ref, o_ref, acc_ref, mask_ref, later_ref, *scratch, tq, n_heads):
    qi = pl.program_id(2)
    l_refs, row_refs, lb_refs, hi_refs, lo_refs = (
        scratch[j * n_heads:(j + 1) * n_heads] for j in range(5))

    @pl.when(qi == 0)
    def _():
        key = lax.broadcasted_iota(jnp.int32, (tq, tq), 0)
        other = lax.broadcasted_iota(jnp.int32, (tq, tq), 1)
        mask_ref[0] = jnp.where(key < other, 1.0, 0.0).astype(F32)
        mask_ref[1] = jnp.ones((tq, tq), F32)
        later_ref[...] = jnp.where(other > key, 1.0, 0.0).astype(BF16)

    feat = lax.broadcasted_iota(jnp.int32, q_ref.shape, 0)
    q = q_ref[...]
    q_heads = [jnp.where((feat >= hh * HEAD_DIM) & (feat < (hh + 1) * HEAD_DIM), q,
                         jnp.zeros_like(q)) for hh in range(n_heads)]

    acc_ref[...] = jnp.zeros_like(acc_ref)
    for hh in range(n_heads):
        l_refs[hh][...] = jnp.zeros_like(l_refs[hh])

    def logits(hh, kb):
        start = pl.multiple_of(kb * tq, tq)
        z = _dot(k_ref[pl.ds(start, tq), :], q_heads[hh])
        sp = jnp.maximum(z, 0.0) + jnp.log(1.0 + jnp.exp(-jnp.abs(z)))
        log_keep = -sp * mask_ref[jnp.where(kb == qi, 0, 1)]
        hi, lo = _split(log_keep)
        hi_refs[hh][...] = hi
        lo_refs[hh][...] = lo
        lb_refs[hh][...] = z - sp
        row_refs[hh][...] = log_keep[:1, :]

    def weights(hh, kb):
        feats = slice(hh * HEAD_DIM, (hh + 1) * HEAD_DIM)
        start = pl.multiple_of(kb * tq, tq)
        later_keys = later_ref[...]
        later = _dot(later_keys, hi_refs[hh][...]) + _dot(later_keys, lo_refs[hh][...])
        carry_l = l_refs[hh][...]
        w = jnp.exp(lb_refs[hh][...] + later + carry_l) * mask_ref[jnp.where(kb == qi, 0, 1)]
        acc_ref[feats, :] += _dot(v_ref[feats, pl.ds(start, tq)], w.astype(BF16))
        l_new = carry_l + later[:1, :] + row_refs[hh][...]
        l_refs[hh][...] = l_new
        return jnp.max(l_new)

    logits(0, qi)

    def cond(c):
        it, lm = c
        return jnp.logical_and(it <= qi, lm > EXP_UNDERFLOW)

    def body(c):
        it, _ = c
        kb = qi - it
        lm = None
        for hh in range(n_heads):
            if hh + 1 < n_heads:
                logits(hh + 1, kb)
            else:
                logits(0, jnp.maximum(kb - 1, 0))
            head_max = weights(hh, kb)
            lm = head_max if lm is None else jnp.maximum(lm, head_max)
        return it + 1, lm

    lax.while_loop(cond, body, (jnp.int32(0), jnp.float32(0.0)))
    o_ref[...] = acc_ref[...].T.astype(o_ref.dtype)


def _sb_call(q, k, v, *, batch, tq, n_heads):
    n, width = k.shape
    seq = n // batch
    nq = seq // tq
    fw = n_heads * HEAD_DIM
    return pl.pallas_call(
        functools.partial(_sb_kernel, tq=tq, n_heads=n_heads),
        grid=(batch, width // fw, nq),
        in_specs=[pl.BlockSpec((fw, tq), lambda b, h, i: (h, b * nq + i)),
                  pl.BlockSpec((seq, fw), lambda b, h, i: (b, h)),
                  pl.BlockSpec((fw, seq), lambda b, h, i: (h, b))],
        out_specs=pl.BlockSpec((tq, fw), lambda b, h, i: (b * nq + i, h)),
        out_shape=jax.ShapeDtypeStruct((n, width), BF16),
        scratch_shapes=[pltpu.VMEM((fw, tq), F32), pltpu.VMEM((2, tq, tq), F32),
                        pltpu.VMEM((tq, tq), BF16)]
        + [pltpu.VMEM((1, tq), F32)] * (2 * n_heads) + [pltpu.VMEM((tq, tq), F32)] * n_heads
        + [pltpu.VMEM((tq, tq), BF16)] * (2 * n_heads),
        compiler_params=_params(("parallel", "parallel", "arbitrary")),
        name="sb_attn",
    )(q, k, v)


def _diff_kernel(slopes_ref, lam_ref, g_ref, q_ref, k_ref, v_ref, o_ref, bias_ref,
                 s0_ref, s1_ref, mb0_ref, mb1_ref, m0_ref, m1_ref, acc0_ref, acc1_ref, kmax_ref,
                 *, tq, tk, lambda_init):
    h = pl.program_id(1)
    qi = pl.program_id(2)
    slope = slopes_ref[0, h]
    inv_slope = slopes_ref[1, h]
    nsub = tk // tq
    s_refs, mb_refs = (s0_ref, s1_ref), (mb0_ref, mb1_ref)
    m_refs, acc_refs = (m0_ref, m1_ref), (acc0_ref, acc1_ref)

    @pl.when(qi == 0)
    def _():
        key = lax.broadcasted_iota(jnp.int32, (tk, tq), 0)
        query = lax.broadcasted_iota(jnp.int32, (tk, tq), 1)
        key_ahead = key - query
        rel_bias = slope * key_ahead.astype(F32)
        bias_ref[nsub] = rel_bias
        for r in range(nsub):
            bias_ref[r] = jnp.where(key_ahead <= r * tq, rel_bias, MASKED)
        kf = k_ref[...].astype(F32)
        kmax_ref[0] = jnp.max(jnp.sum(kf * kf, axis=1, keepdims=True))

    feat = lax.broadcasted_iota(jnp.int32, (LANES, tq), 0)
    q = q_ref[...]
    q_maps = [jnp.where((feat >= c * HEAD_DIM) & (feat < (c + 1) * HEAD_DIM), q, jnp.zeros_like(q))
              for c in range(2)]

    for c in range(2):
        m_refs[c][...] = jnp.full_like(m_refs[c], MASKED)
        acc_refs[c][...] = jnp.zeros_like(acc_refs[c])

    last = (qi * tq) // tk

    qf = q.astype(F32)
    qk = jnp.max(jnp.sqrt(jnp.sum(qf * qf, axis=0, keepdims=True) * kmax_ref[0]))
    skip_dist = (EXP2_UNDERFLOW + 2.01 * qk) * inv_slope
    first = jnp.clip(((qi * tq).astype(F32) - skip_dist) * (1.0 / tk), 0.0, last.astype(F32))
    first = first.astype(jnp.int32)

    def offset(kb):
        return -slope * (qi * tq - kb * tk).astype(F32)

    half = tk // 2

    def scores(c, kb):
        sel = jnp.where(kb == last, qi % nsub, nsub)
        mb = None
        for part in range(2):
            rows = pl.ds(part * half, half)
            start = pl.multiple_of(kb * tk + part * half, half)
            s = _dot(k_ref[pl.ds(start, half), :], q_maps[c]) + bias_ref[sel, rows, :]
            s_refs[c][rows, :] = s
            part_max = jnp.max(s, axis=0, keepdims=True)
            mb = part_max if mb is None else jnp.maximum(mb, part_max)
        mb_refs[c][...] = mb + offset(kb)

    def accumulate(c, kb):
        m_old = m_refs[c][...]
        m_new = jnp.maximum(m_old, mb_refs[c][...])
        alpha = jnp.exp2(m_old - m_new)
        shift = m_new - offset(kb)
        pv = None
        for part in range(2):
            start = pl.multiple_of(kb * tk + part * half, half)
            p = jnp.exp2(s_refs[c][pl.ds(part * half, half), :] - shift)
            part_pv = _dot(v_ref[:, pl.ds(start, half)], p.astype(BF16))
            pv = part_pv if pv is None else pv + part_pv
        acc_refs[c][...] = alpha * acc_refs[c][...] + pv
        m_refs[c][...] = m_new

    scores(0, first)

    def step(kb):
        scores(1, kb)
        accumulate(0, kb)
        scores(0, jnp.minimum(kb + 1, last))
        accumulate(1, kb)

    def pair(j, carry):
        step(first + 2 * j)
        step(first + 2 * j + 1)
        return carry

    n_blocks = last + 1 - first
    lax.fori_loop(0, n_blocks // 2, pair, 0)

    @pl.when(n_blocks % 2 == 1)
    def _():
        step(last)

    lp = lam_ref[...]
    lam = (jnp.exp(jnp.sum(lp[0:1] * lp[1:2], axis=-1, keepdims=True))
           - jnp.exp(jnp.sum(lp[2:3] * lp[3:4], axis=-1, keepdims=True)) + lambda_init)
    a0, a1 = acc0_ref[...], acc1_ref[...]
    o = (a0[:LANES] * (1.0 / a0[LANES:LANES + 1])
         - lam * (a1[:LANES] * (1.0 / a1[LANES:LANES + 1])))
    ms = jnp.mean(o * o, axis=0, keepdims=True)
    o = ((o * lax.rsqrt(ms + NORM_EPS)) * g_ref[...]) * (1.0 - lambda_init)
    o_ref[...] = o.T.astype(o_ref.dtype)


def _diff_call(q, k, v, lam_params, subln_g, *, batch, tq, tk, lambda_init):
    n, width = k.shape
    heads = width // LANES
    assert tk % tq == 0
    slopes = [2.0 ** (-8.0 * (h + 1) / heads) * LOG2E for h in range(heads)]
    slopes = jnp.asarray([slopes, [1.0 / s for s in slopes]], dtype=F32)
    seq = n // batch
    q_spec, k_spec, _, o_spec = _attn_specs(batch, seq, tq)
    v_spec = pl.BlockSpec((V_ROWS, seq), lambda b, h, i: (h, b))
    return pl.pallas_call(
        functools.partial(_diff_kernel, tq=tq, tk=tk, lambda_init=lambda_init),
        grid=(batch, heads, seq // tq),
        in_specs=[pl.BlockSpec(memory_space=pltpu.SMEM),
                  pl.BlockSpec(lam_params.shape, lambda b, h, i: (0, 0)),
                  pl.BlockSpec((LANES, 1), lambda b, h, i: (0, 0)),
                  q_spec, k_spec, v_spec],
        out_specs=o_spec,
        out_shape=jax.ShapeDtypeStruct((n, width), BF16),
        scratch_shapes=[pltpu.VMEM((tk // tq + 1, tk, tq), F32)]
        + [pltpu.VMEM((tk, tq), F32)] * 2 + [pltpu.VMEM((1, tq), F32)] * 4
        + [pltpu.VMEM((V_ROWS, tq), F32)] * 2 + [pltpu.SMEM((1,), F32)],
        compiler_params=_params(("parallel", "parallel", "arbitrary")),
        name="diff_attn",
    )(slopes, lam_params.astype(F32), subln_g.reshape(LANES, 1).astype(F32), q, k, v)


def _merge_kernel(*refs, router):
    (x_ref, ysb_ref, ydf_ref, gmix_ref, wg_ref, bg_ref, wsb_ref, wdf_ref, wo_ref,
     gffn_ref) = refs[:10]
    rest = refs[10:]
    x = x_ref[...]
    d = x.shape[-1]
    h = _rms(x, gmix_ref[...]).astype(BF16)
    gates = jax.nn.sigmoid(_dot(h, wg_ref[...]) + bg_ref[...])
    merged = (gates[:, :d] * _dot(ysb_ref[...], wsb_ref[...])
              + gates[:, d:] * _dot(ydf_ref[...], wdf_ref[...]))
    xn = x + _dot(merged.astype(BF16), wo_ref[...])
    h2 = _rms(xn, gffn_ref[...])
    if router:
        wr_ref, xo_ref, h2_ref, lg_ref = rest
        lg_ref[...] = _dot(h2.astype(BF16), wr_ref[...])
    else:
        xo_ref, h2_ref = rest
    xo_ref[...] = xn
    h2_ref[...] = h2.astype(h2_ref.dtype)


def _merge_call(x, ysb, ydf, gmix, wg, bg, wsb, wdf, wo, gffn, w_router, *, tm):
    n, d = x.shape
    router = w_router is not None
    row = lambda w: pl.BlockSpec((tm, w), lambda i: (i, 0))
    const = lambda a: pl.BlockSpec(a.shape, lambda i: (0, 0))
    args = [x, ysb, ydf, gmix.reshape(1, d), wg, bg.reshape(1, -1), wsb, wdf, wo,
            gffn.reshape(1, d)]
    in_specs = [row(d), row(ysb.shape[1]), row(ydf.shape[1])] + [const(a) for a in args[3:]]
    out_specs = [row(d), row(d)]
    out_shape = [jax.ShapeDtypeStruct((n, d), F32),
                 jax.ShapeDtypeStruct((n, d), F32 if router else BF16)]
    if router:
        wr = jnp.zeros((d, LANES), BF16).at[:, :w_router.shape[1]].set(w_router.astype(BF16))
        args.append(wr)
        in_specs.append(const(wr))
        out_specs.append(row(LANES))
        out_shape.append(jax.ShapeDtypeStruct((n, LANES), F32))
    return pl.pallas_call(
        functools.partial(_merge_kernel, router=router),
        grid=(n // tm,),
        in_specs=in_specs,
        out_specs=out_specs,
        out_shape=out_shape,
        compiler_params=_params(("parallel",)),
        name="merge_router" if router else "merge",
    )(*args)


def _swiglu_kernel(x_ref, h_ref, wg_ref, wu_ref, wd_ref, o_ref, acc_ref):
    f = pl.program_id(1)

    @pl.when(f == 0)
    def _():
        acc_ref[...] = jnp.zeros_like(acc_ref)

    h = h_ref[...]
    g = _dot(h, wg_ref[...])
    u = _dot(h, wu_ref[...])
    act = (g * jax.nn.sigmoid(g)) * u
    acc_ref[...] += _dot(act.astype(BF16), wd_ref[...])

    @pl.when(f == pl.num_programs(1) - 1)
    def _():
        o_ref[...] = x_ref[...] + acc_ref[...]


def _swiglu_call(x, h, w_gate_up, w_down, *, tm, nf):
    n, d = x.shape
    d_ff = w_down.shape[0]
    tf = d_ff // nf
    assert tf * nf == d_ff and tf % LANES == 0
    row = pl.BlockSpec((tm, d), lambda i, f: (i, 0))
    return pl.pallas_call(
        _swiglu_kernel,
        grid=(n // tm, nf),
        in_specs=[row, row,
                  pl.BlockSpec((d, tf), lambda i, f: (0, f)),
                  pl.BlockSpec((d, tf), lambda i, f: (0, f + nf)),
                  pl.BlockSpec((tf, d), lambda i, f: (f, 0))],
        out_specs=row,
        out_shape=jax.ShapeDtypeStruct((n, d), F32),
        scratch_shapes=[pltpu.VMEM((tm, d), F32)],
        compiler_params=_params(("parallel", "arbitrary")),
        name="swiglu",
    )(x, h, w_gate_up, w_gate_up, w_down)


def _expert_kernel(be_ref, tok_ref, dst_ref, meta_ref, h_hbm, sw_ref, w1_ref, w3_ref, w2_ref,
                   y_hbm, xbuf, xb_ref, acc_ref, ybuf, zrow_ref, gsem, ssem, *, n_spare):
    del be_ref
    i = pl.program_id(0)
    f = pl.program_id(1)
    nf = pl.num_programs(1)
    nused = meta_ref[0]
    rows = xb_ref.shape[0]

    def gather_copy(blk, r):
        return pltpu.make_async_copy(h_hbm.at[pl.ds(tok_ref[blk * rows + r], 1)],
                                     xbuf.at[blk % 2, pl.ds(r, 1)], gsem.at[blk % 2])

    def scatter_copy(blk, r):
        return pltpu.make_async_copy(ybuf.at[(blk + 2) % 2, pl.ds(r, 1)],
                                     y_hbm.at[pl.ds(dst_ref[(blk + 1) * rows + r], 1)],
                                     ssem.at[(blk + 2) % 2])

    def for_rows(lo, n, fn):
        def body(r, carry):
            fn(lo + r)
            return carry
        lax.fori_loop(0, n, body, 0, unroll=8)

    @pl.when(jnp.logical_and(i == 0, f == 0))
    def _():
        ybuf[1] = jnp.zeros(ybuf.shape[1:], ybuf.dtype)
        for_rows(0, rows, lambda r: gather_copy(0, r).start())

    @pl.when(jnp.logical_and(f == 0, i <= nused))
    def _():
        for_rows(0, rows, lambda r: gather_copy(i, r).wait())

    @pl.when(jnp.logical_and(f == 0, i < nused))
    def _():
        xb_ref[...] = xbuf[i % 2].astype(BF16)
        acc_ref[...] = jnp.zeros_like(acc_ref)

    chunk = rows // nf

    @pl.when(i < nused)
    def _():
        for r in range(chunk):
            gather_copy(i + 1, f * chunk + r).start()
            scatter_copy(i - 1, f * chunk + r).start()
        xb = xb_ref[...]
        g = _dot(xb, w1_ref[...])
        u = _dot(xb, w3_ref[...])
        hid = (g * jax.nn.sigmoid(g)) * u
        acc_ref[...] += _dot(hid.astype(BF16), w2_ref[...])

    @pl.when(i == nused)
    def _():
        for_rows(f * chunk, chunk, lambda r: scatter_copy(i - 1, r).start())

    @pl.when(f == nf - 1)
    def _():
        @pl.when(jnp.logical_and(i < nused, i >= 1))
        def _():
            for_rows(0, rows, lambda r: scatter_copy(i - 2, r).wait())

        @pl.when(i < nused)
        def _():
            ybuf[i % 2] = acc_ref[...] * sw_ref[...]

        @pl.when(i == nused)
        def _():
            @pl.when(i >= 1)
            def _():
                for_rows(0, rows, lambda r: scatter_copy(i - 2, r).wait())
            for_rows(0, rows, lambda r: scatter_copy(i - 1, r).wait())

            zrow_ref[...] = jnp.zeros_like(zrow_ref)
            n_pad = meta_ref[1]
            spare_lo = y_hbm.shape[0] - rows - n_spare

            def zero_copy(r):
                return pltpu.make_async_copy(zrow_ref.at[pl.ds(0, 1)],
                                             y_hbm.at[pl.ds(spare_lo + r, 1)], ssem.at[0])

            def fill(r, carry):
                zero_copy(r).start()
                return carry

            def drain(r, carry):
                zero_copy(r).wait()
                return carry

            lax.fori_loop(n_pad, n_spare, fill, 0)
            lax.fori_loop(n_pad, n_spare, drain, 0)


def _expert_call(block_e, slot_tok, dst_all, meta, h, slot_w, w1, w3, w2, *, n_spare, nf):
    n_slots = slot_tok.shape[0]
    n_out = h.shape[0] * TOP_K + n_spare + MOE_BLOCK
    n_blocks = n_slots // MOE_BLOCK
    d = h.shape[1]
    d_ff = w1.shape[2]
    tf = d_ff // nf
    assert tf * nf == d_ff and tf % LANES == 0 and MOE_BLOCK % nf == 0
    def tile(i, f, meta):
        return jnp.where(i < meta[0], f, nf - 1)

    grid_spec = pltpu.PrefetchScalarGridSpec(
        num_scalar_prefetch=4,
        grid=(n_blocks, nf),
        in_specs=[pl.BlockSpec(memory_space=pl.ANY),
                  pl.BlockSpec((MOE_BLOCK, 1), lambda i, f, be, *_: (i, 0)),
                  pl.BlockSpec((None, d, tf), lambda i, f, be, tok, dst, meta:
                               (be[i], 0, tile(i, f, meta))),
                  pl.BlockSpec((None, d, tf), lambda i, f, be, tok, dst, meta:
                               (be[i], 0, tile(i, f, meta))),
                  pl.BlockSpec((None, tf, d), lambda i, f, be, tok, dst, meta:
                               (be[i], tile(i, f, meta), 0))],
        out_specs=pl.BlockSpec(memory_space=pl.ANY),
        scratch_shapes=[pltpu.VMEM((2, MOE_BLOCK, d), F32),
                        pltpu.VMEM((MOE_BLOCK, d), BF16),
                        pltpu.VMEM((MOE_BLOCK, d), F32),
                        pltpu.VMEM((2, MOE_BLOCK, d), F32),
                        pltpu.VMEM((8, d), F32),
                        pltpu.SemaphoreType.DMA((2,)),
                        pltpu.SemaphoreType.DMA((2,))],
    )
    return pl.pallas_call(
        functools.partial(_expert_kernel, n_spare=n_spare),
        grid_spec=grid_spec,
        out_shape=jax.ShapeDtypeStruct((n_out, d), F32),
        compiler_params=_params(("arbitrary", "arbitrary")),
        name="moe_experts",
    )(block_e, slot_tok, dst_all, meta, h, slot_w, w1, w3, w2)


def _combine_kernel(x_ref, y0_ref, y1_ref, o_ref):
    o_ref[...] = x_ref[...] + (y0_ref[...] + y1_ref[...])


def _combine_call(x, y, *, tm):
    n, d = x.shape
    steps = n // tm
    return pl.pallas_call(
        _combine_kernel,
        grid=(steps,),
        in_specs=[pl.BlockSpec((tm, d), lambda i: (i, 0)),
                  pl.BlockSpec((tm, d), lambda i: (i, 0)),
                  pl.BlockSpec((tm, d), lambda i: (i + steps, 0))],
        out_specs=pl.BlockSpec((tm, d), lambda i: (i, 0)),
        out_shape=jax.ShapeDtypeStruct((n, d), F32),
        compiler_params=_params(("parallel",)),
        name="moe_combine",
    )(x, y, y)


def _route(logits):
    n_tok = logits.shape[0]
    n_assign = n_tok * TOP_K
    n_slots = -(-n_assign // MOE_BLOCK) * MOE_BLOCK + N_EXPERTS * MOE_BLOCK
    n_blocks = n_slots // MOE_BLOCK
    top_logit, top_idx = lax.top_k(logits, TOP_K)
    top_w = jax.nn.softmax(top_logit, axis=-1)
    onehot = jnp.sum(jax.nn.one_hot(top_idx, N_EXPERTS, dtype=jnp.int32), axis=1)
    csum = jnp.cumsum(onehot, axis=0)
    counts = csum[-1]
    padded = (counts + MOE_BLOCK - 1) // MOE_BLOCK * MOE_BLOCK
    padded_end = jnp.cumsum(padded)
    padded_start = padded_end - padded
    rank = jnp.take_along_axis(csum - onehot, top_idx, axis=1)
    slots = (padded_start[top_idx] + rank).astype(jnp.int32)
    dst = (jnp.arange(TOP_K, dtype=jnp.int32)[None, :] * n_tok
           + jnp.arange(n_tok, dtype=jnp.int32)[:, None])
    slot_dst = jnp.full((n_slots,), -1, jnp.int32).at[slots.reshape(-1)].set(dst.reshape(-1))
    real = slot_dst >= 0
    n_spare = n_slots - n_assign
    pad_rank = jnp.cumsum(jnp.logical_not(real).astype(jnp.int32)) - 1
    slot_tok = jnp.where(real, slot_dst % n_tok, 0)
    w_kmajor = top_w.T.reshape(-1)
    slot_w = jnp.where(real, w_kmajor[jnp.maximum(slot_dst, 0)], 0.0)
    slot_dst = jnp.where(real, slot_dst, n_assign + pad_rank)
    virtual = n_assign + n_spare + jnp.arange(MOE_BLOCK, dtype=jnp.int32)
    dst_all = jnp.concatenate([virtual, slot_dst])
    block_e = jnp.minimum(
        jnp.searchsorted(padded_end, jnp.arange(n_blocks) * MOE_BLOCK, side='right'),
        N_EXPERTS - 1).astype(jnp.int32)
    meta = jnp.stack([padded_end[-1] // MOE_BLOCK, padded_end[-1] - n_assign]).astype(jnp.int32)
    return block_e, slot_tok, dst_all, meta, slot_w.reshape(n_slots, 1), n_spare


def kernel(x, norm_mix_g, w_in, diff_q_norm_g, diff_k_norm_g, diff_lambda, diff_subln_g,
           w_gate, b_gate, w_branch_sb, w_branch_diff, w_out, norm_ffn_g,
           ffn_w_gate_up, ffn_w_down, moe_w_router, moe_w1, moe_w3, moe_w2):
    batch, seq, d = x.shape
    n = batch * seq
    depth = w_in.shape[0]
    tm = min(512, n)
    tq = min(256, seq)
    xf = x.reshape(n, d).astype(F32)
    bf = lambda w: w.astype(BF16)
    for i in range(depth):
        sbk, dfk, sbq, sbv, dfq, dfv = _proj_call(
            xf, norm_mix_g[i], w_in[i], diff_q_norm_g[i], diff_k_norm_g[i], tm=tm)
        y_sb = _sb_call(sbq, sbk, sbv, batch=batch, tq=tq, n_heads=4)
        lambda_init = 0.8 - 0.6 * math.exp(-0.3 * i)
        y_df = _diff_call(dfq, dfk, dfv, diff_lambda[i], diff_subln_g[i],
                          batch=batch, tq=min(512, seq), tk=min(1024, seq),
                          lambda_init=lambda_init)
        j = i // 2
        dense = i % 2 == 0
        outs = _merge_call(xf, y_sb, y_df, norm_mix_g[i], bf(w_gate[i]), b_gate[i],
                           bf(w_branch_sb[i]), bf(w_branch_diff[i]), bf(w_out[i]),
                           norm_ffn_g[i], None if dense else moe_w_router[j], tm=tm)
        if dense:
            xf, h2 = outs
            xf = _swiglu_call(xf, h2, bf(ffn_w_gate_up[j]), bf(ffn_w_down[j]), tm=tm, nf=2)
        else:
            xf, h2, logits = outs
            block_e, slot_tok, dst_all, meta, slot_w, n_spare = _route(logits[:, :N_EXPERTS])
            y = _expert_call(block_e, slot_tok, dst_all, meta, h2, slot_w,
                             bf(moe_w1[j]), bf(moe_w3[j]), bf(moe_w2[j]), n_spare=n_spare, nf=2)
            xf = _combine_call(xf, y, tm=tm)
    return xf.reshape(batch, seq, d).astype(x.dtype)
```

```python
import functools
import math

import jax
import jax.numpy as jnp
from jax import lax
from jax.experimental import pallas as pl
from jax.experimental.pallas import tpu as pltpu

F32 = jnp.float32
BF16 = jnp.bfloat16

HEAD_DIM = 64
NORM_EPS = 1e-6
N_EXPERTS = 8
TOP_K = 2
MOE_BLOCK = 512
LANES = 128
TOKEN_TILE_ROWS = 8
VMEM_LIMIT = 56 * 1024 * 1024

EXP_UNDERFLOW = -104.0
EXP2_UNDERFLOW = 152.0
MASKED = -1e30
LOG2E = math.log2(math.e)
V_ROWS = LANES + 16


def _dot(a, b):
    return jnp.dot(a, b, preferred_element_type=F32)


def _dot_nt(a, b):
    return lax.dot_general(a, b, (((1,), (1,)), ((), ())), preferred_element_type=F32)


def _rms(x, g):
    ms = jnp.mean(x * x, axis=-1, keepdims=True)
    return (x * lax.rsqrt(ms + NORM_EPS)) * g


def _split(x):
    hi = x.astype(BF16)
    return hi, (x - hi.astype(F32)).astype(BF16)


def _store_token_tiles(ref, x):
    rows = x.shape[0]
    for j in range(TOKEN_TILE_ROWS):
        ref[pl.ds(j, rows, stride=TOKEN_TILE_ROWS), :] = x[:, j * LANES:(j + 1) * LANES]


def _load_token_tiles(ref, rows):
    return jnp.concatenate([ref[pl.ds(j, rows, stride=TOKEN_TILE_ROWS), :]
                            for j in range(TOKEN_TILE_ROWS)], axis=1)


def _params(sem, vmem=VMEM_LIMIT):
    return pltpu.CompilerParams(dimension_semantics=sem, vmem_limit_bytes=vmem)


def _proj_kernel(x_ref, g_ref, wk_ref, wt_ref, bd_ref, gq_ref, gk_ref,
                 sbk_ref, dfk_ref, sbq_ref, sbv_ref, dfq_ref, dfv_ref, *, width):
    scale = HEAD_DIM ** -0.5
    h = _rms(x_ref[...], g_ref[...]).astype(BF16)
    bd = bd_ref[...]

    def feature_major(j):
        return _dot_nt(wt_ref[j * width:(j + 1) * width, :], h)

    sbk_ref[...] = _dot(h, wk_ref[:, :width]).astype(BF16)
    k = _dot(h, wk_ref[:, width:])
    hi, lo = _split(k * k)
    ss = _dot(hi, bd) + _dot(lo, bd)
    dfk_ref[...] = ((k * lax.rsqrt(ss * (1.0 / HEAD_DIM) + NORM_EPS)) * gk_ref[...]).astype(BF16)

    sbq_ref[...] = (feature_major(0) * scale).astype(BF16)
    sbv_ref[...] = feature_major(1).astype(BF16)
    q = feature_major(2)
    hi, lo = _split(q * q)
    ss = _dot(bd, hi) + _dot(bd, lo)
    q = (q * lax.rsqrt(ss * (1.0 / HEAD_DIM) + NORM_EPS)) * gq_ref[...]
    dfq_ref[...] = (q * (scale * LOG2E)).astype(BF16)
    v = feature_major(3).astype(BF16)
    for hd in range(width // LANES):
        dfv_ref[hd * V_ROWS:hd * V_ROWS + LANES, :] = v[hd * LANES:(hd + 1) * LANES, :]
        dfv_ref[hd * V_ROWS + LANES:(hd + 1) * V_ROWS, :] = jnp.ones(
            (V_ROWS - LANES, v.shape[1]), BF16)


def _proj_call(x, g, w_in, gq, gk, *, tm):
    n, d = x.shape
    width = w_in.shape[1] // 6
    cols = lambda j: w_in[:, j * width:(j + 1) * width]
    wk = jnp.concatenate([cols(1), cols(4)], axis=1).astype(BF16)
    wt = jnp.concatenate([cols(0), cols(2), cols(3), cols(5)], axis=1).T.astype(BF16)
    grp = lax.broadcasted_iota(jnp.int32, (width, width), 0) // HEAD_DIM
    bd = (grp == grp.T).astype(BF16)
    const = lambda a: pl.BlockSpec(a.shape, lambda i: (0, 0))
    tok_major = pl.BlockSpec((tm, width), lambda i: (i, 0))
    feat_major = pl.BlockSpec((width, tm), lambda i: (0, i))
    v_rows = width // LANES * V_ROWS
    args = [x, g.reshape(1, d), wk, wt, bd,
            jnp.tile(gq, width // HEAD_DIM).reshape(width, 1),
            jnp.tile(gk, width // HEAD_DIM).reshape(1, width)]
    return pl.pallas_call(
        functools.partial(_proj_kernel, width=width),
        grid=(n // tm,),
        in_specs=[pl.BlockSpec((tm, d), lambda i: (i, 0))] + [const(a) for a in args[1:]],
        out_specs=[tok_major] * 2 + [feat_major] * 3 + [pl.BlockSpec((v_rows, tm), lambda i: (0, i))],
        out_shape=[jax.ShapeDtypeStruct((n, width), BF16)] * 2
        + [jax.ShapeDtypeStruct((width, n), BF16)] * 3 + [jax.ShapeDtypeStruct((v_rows, n), BF16)],
        compiler_params=_params(("parallel",)),
        name="proj",
    )(*args)


def _attn_specs(batch, seq, tq):
    nq = seq // tq
    q_spec = pl.BlockSpec((LANES, tq), lambda b, h, i: (h, b * nq + i))
    k_spec = pl.BlockSpec((seq, LANES), lambda b, h, i: (b, h))
    v_spec = pl.BlockSpec((LANES, seq), lambda b, h, i: (h, b))
    o_spec = pl.BlockSpec((tq, LANES), lambda b, h, i: (b * nq + i, h))
    return q_spec, k_spec, v_spec, o_spec


def _sb_kernel(q_ref, k_ref, v_ref, o_ref, acc_ref, mask_ref, later_ref, *scratch, tq, n_heads):
    qi = pl.program_id(2)
    l_refs, row_refs, lb_refs, hi_refs, lo_refs = (
        scratch[j * n_heads:(j + 1) * n_heads] for j in range(5))

    @pl.when(qi == 0)
    def _():
        key = lax.broadcasted_iota(jnp.int32, (tq, tq), 0)
        other = lax.broadcasted_iota(jnp.int32, (tq, tq), 1)
        mask_ref[0] = jnp.where(key < other, 1.0, 0.0).astype(F32)
        mask_ref[1] = jnp.ones((tq, tq), F32)
        later_ref[...] = jnp.where(other > key, 1.0, 0.0).astype(BF16)

    feat = lax.broadcasted_iota(jnp.int32, q_ref.shape, 0)
    q = q_ref[...]
    q_heads = [jnp.where((feat >= hh * HEAD_DIM) & (feat < (hh + 1) * HEAD_DIM), q,
                         jnp.zeros_like(q)) for hh in range(n_heads)]

    acc_ref[...] = jnp.zeros_like(acc_ref)
    for hh in range(n_heads):
        l_refs[hh][...] = jnp.zeros_like(l_refs[hh])

    def logits(hh, kb):
        start = pl.multiple_of(kb * tq, tq)
        z = _dot(k_ref[pl.ds(start, tq), :], q_heads[hh])
        sp = jnp.maximum(z, 0.0) + jnp.log(1.0 + jnp.exp(-jnp.abs(z)))
        log_keep = -sp * mask_ref[jnp.where(kb == qi, 0, 1)]
        hi, lo = _split(log_keep)
        hi_refs[hh][...] = hi
        lo_refs[hh][...] = lo
        lb_refs[hh][...] = z - sp
        row_refs[hh][...] = log_keep[:1, :]

    def weights(hh, kb):
        feats = slice(hh * HEAD_DIM, (hh + 1) * HEAD_DIM)
        start = pl.multiple_of(kb * tq, tq)
        later_keys = later_ref[...]
        later = _dot(later_keys, hi_refs[hh][...]) + _dot(later_keys, lo_refs[hh][...])
        carry_l = l_refs[hh][...]
        w = jnp.exp(lb_refs[hh][...] + later + carry_l) * mask_ref[jnp.where(kb == qi, 0, 1)]
        acc_ref[feats, :] += _dot(v_ref[feats, pl.ds(start, tq)], w.astype(BF16))
        l_new = carry_l + later[:1, :] + row_refs[hh][...]
        l_refs[hh][...] = l_new
        return jnp.max(l_new)

    logits(0, qi)

    def cond(c):
        it, lm = c
        return jnp.logical_and(it <= qi, lm > EXP_UNDERFLOW)

    def body(c):
        it, _ = c
        kb = qi - it
        lm = None
        for hh in range(n_heads):
            if hh + 1 < n_heads:
                logits(hh + 1, kb)
            else:
                logits(0, jnp.maximum(kb - 1, 0))
            head_max = weights(hh, kb)
            lm = head_max if lm is None else jnp.maximum(lm, head_max)
        return it + 1, lm

    lax.while_loop(cond, body, (jnp.int32(0), jnp.float32(0.0)))
    o_ref[...] = acc_ref[...].T.astype(o_ref.dtype)


def _sb_call(q, k, v, *, batch, tq, n_heads):
    n, width = k.shape
    seq = n // batch
    nq = seq // tq
    fw = n_heads * HEAD_DIM
    return pl.pallas_call(
        functools.partial(_sb_kernel, tq=tq, n_heads=n_heads),
        grid=(batch, width // fw, nq),
        in_specs=[pl.BlockSpec((fw, tq), lambda b, h, i: (h, b * nq + i)),
                  pl.BlockSpec((seq, fw), lambda b, h, i: (b, h)),
                  pl.BlockSpec((fw, seq), lambda b, h, i: (h, b))],
        out_specs=pl.BlockSpec((tq, fw), lambda b, h, i: (b * nq + i, h)),
        out_shape=jax.ShapeDtypeStruct((n, width), BF16),
        scratch_shapes=[pltpu.VMEM((fw, tq), F32), pltpu.VMEM((2, tq, tq), F32),
                        pltpu.VMEM((tq, tq), BF16)]
        + [pltpu.VMEM((1, tq), F32)] * (2 * n_heads) + [pltpu.VMEM((tq, tq), F32)] * n_heads
        + [pltpu.VMEM((tq, tq), BF16)] * (2 * n_heads),
        compiler_params=_params(("parallel", "parallel", "arbitrary")),
        name="sb_attn",
    )(q, k, v)


def _diff_kernel(slopes_ref, lam_ref, g_ref, q_ref, k_ref, v_ref, o_ref, bias_ref,
                 s0_ref, s1_ref, mb0_ref, mb1_ref, m0_ref, m1_ref, acc0_ref, acc1_ref, kmax_ref,
                 *, tq, tk, lambda_init):
    h = pl.program_id(1)
    qi = pl.program_id(2)
    slope = slopes_ref[0, h]
    inv_slope = slopes_ref[1, h]
    nsub = tk // tq
    s_refs, mb_refs = (s0_ref, s1_ref), (mb0_ref, mb1_ref)
    m_refs, acc_refs = (m0_ref, m1_ref), (acc0_ref, acc1_ref)

    @pl.when(qi == 0)
    def _():
        key = lax.broadcasted_iota(jnp.int32, (tk, tq), 0)
        query = lax.broadcasted_iota(jnp.int32, (tk, tq), 1)
        key_ahead = key - query
        rel_bias = slope * key_ahead.astype(F32)
        bias_ref[nsub] = rel_bias
        for r in range(nsub):
            bias_ref[r] = jnp.where(key_ahead <= r * tq, rel_bias, MASKED)
        kf = k_ref[...].astype(F32)
        kmax_ref[0] = jnp.max(jnp.sum(kf * kf, axis=1, keepdims=True))

    feat = lax.broadcasted_iota(jnp.int32, (LANES, tq), 0)
    q = q_ref[...]
    q_maps = [jnp.where((feat >= c * HEAD_DIM) & (feat < (c + 1) * HEAD_DIM), q, jnp.zeros_like(q))
              for c in range(2)]

    for c in range(2):
        m_refs[c][...] = jnp.full_like(m_refs[c], MASKED)
        acc_refs[c][...] = jnp.zeros_like(acc_refs[c])

    last = (qi * tq) // tk

    qf = q.astype(F32)
    qk = jnp.max(jnp.sqrt(jnp.sum(qf * qf, axis=0, keepdims=True) * kmax_ref[0]))
    skip_dist = (EXP2_UNDERFLOW + 2.01 * qk) * inv_slope
    first = jnp.clip(((qi * tq).astype(F32) - skip_dist) * (1.0 / tk), 0.0, last.astype(F32))
    first = first.astype(jnp.int32)

    def offset(kb):
        return -slope * (qi * tq - kb * tk).astype(F32)

    half = tk // 2

    def scores(c, kb):
        sel = jnp.where(kb == last, qi % nsub, nsub)
        mb = None
        for part in range(2):
            rows = pl.ds(part * half, half)
            start = pl.multiple_of(kb * tk + part * half, half)
            s = _dot(k_ref[pl.ds(start, half), :], q_maps[c]) + bias_ref[sel, rows, :]
            s_refs[c][rows, :] = s
            part_max = jnp.max(s, axis=0, keepdims=True)
            mb = part_max if mb is None else jnp.maximum(mb, part_max)
        mb_refs[c][...] = mb + offset(kb)

    def accumulate(c, kb):
        m_old = m_refs[c][...]
        m_new = jnp.maximum(m_old, mb_refs[c][...])
        alpha = jnp.exp2(m_old - m_new)
        shift = m_new - offset(kb)
        pv = None
        for part in range(2):
            start = pl.multiple_of(kb * tk + part * half, half)
            p = jnp.exp2(s_refs[c][pl.ds(part * half, half), :] - shift)
            part_pv = _dot(v_ref[:, pl.ds(start, half)], p.astype(BF16))
            pv = part_pv if pv is None else pv + part_pv
        acc_refs[c][...] = alpha * acc_refs[c][...] + pv
        m_refs[c][...] = m_new

    scores(0, first)

    def step(kb):
        scores(1, kb)
        accumulate(0, kb)
        scores(0, jnp.minimum(kb + 1, last))
        accumulate(1, kb)

    def pair(j, carry):
        step(first + 2 * j)
        step(first + 2 * j + 1)
        return carry

    n_blocks = last + 1 - first
    lax.fori_loop(0, n_blocks // 2, pair, 0)

    @pl.when(n_blocks % 2 == 1)
    def _():
        step(last)

    lp = lam_ref[...]
    lam = (jnp.exp(jnp.sum(lp[0:1] * lp[1:2], axis=-1, keepdims=True))
           - jnp.exp(jnp.sum(lp[2:3] * lp[3:4], axis=-1, keepdims=True)) + lambda_init)
    a0, a1 = acc0_ref[...], acc1_ref[...]
    o = (a0[:LANES] * (1.0 / a0[LANES:LANES + 1])
         - lam * (a1[:LANES] * (1.0 / a1[LANES:LANES + 1])))
    ms = jnp.mean(o * o, axis=0, keepdims=True)
    o = ((o * lax.rsqrt(ms + NORM_EPS)) * g_ref[...]) * (1.0 - lambda_init)
    o_ref[...] = o.T.astype(o_ref.dtype)


def _diff_call(q, k, v, lam_params, subln_g, *, batch, tq, tk, lambda_init):
    n, width = k.shape
    heads = width // LANES
    assert tk % tq == 0
    slopes = [2.0 ** (-8.0 * (h + 1) / heads) * LOG2E for h in range(heads)]
    slopes = jnp.asarray([slopes, [1.0 / s for s in slopes]], dtype=F32)
    seq = n // batch
    q_spec, k_spec, _, o_spec = _attn_specs(batch, seq, tq)
    v_spec = pl.BlockSpec((V_ROWS, seq), lambda b, h, i: (h, b))
    return pl.pallas_call(
        functools.partial(_diff_kernel, tq=tq, tk=tk, lambda_init=lambda_init),
        grid=(batch, heads, seq // tq),
        in_specs=[pl.BlockSpec(memory_space=pltpu.SMEM),
                  pl.BlockSpec(lam_params.shape, lambda b, h, i: (0, 0)),
                  pl.BlockSpec((LANES, 1), lambda b, h, i: (0, 0)),
                  q_spec, k_spec, v_spec],
        out_specs=o_spec,
        out_shape=jax.ShapeDtypeStruct((n, width), BF16),
        scratch_shapes=[pltpu.VMEM((tk // tq + 1, tk, tq), F32)]
        + [pltpu.VMEM((tk, tq), F32)] * 2 + [pltpu.VMEM((1, tq), F32)] * 4
        + [pltpu.VMEM((V_ROWS, tq), F32)] * 2 + [pltpu.SMEM((1,), F32)],
        compiler_params=_params(("parallel", "parallel", "arbitrary")),
        name="diff_attn",
    )(slopes, lam_params.astype(F32), subln_g.reshape(LANES, 1).astype(F32), q, k, v)


def _merge_kernel(*refs, router):
    (x_ref, ysb_ref, ydf_ref, gmix_ref, wg_ref, bg_ref, wsb_ref, wdf_ref, wo_ref,
     gffn_ref) = refs[:10]
    rest = refs[10:]
    x = x_ref[...]
    d = x.shape[-1]
    h = _rms(x, gmix_ref[...]).astype(BF16)
    gates = jax.nn.sigmoid(_dot(h, wg_ref[...]) + bg_ref[...])
    merged = (gates[:, :d] * _dot(ysb_ref[...], wsb_ref[...])
              + gates[:, d:] * _dot(ydf_ref[...], wdf_ref[...]))
    xn = x + _dot(merged.astype(BF16), wo_ref[...])
    h2 = _rms(xn, gffn_ref[...])
    if router:
        wr_ref, xo_ref, h2_ref, lg_ref = rest
        lg_ref[...] = _dot(h2.astype(BF16), wr_ref[...])
        _store_token_tiles(h2_ref, h2)
    else:
        xo_ref, h2_ref = rest
        h2_ref[...] = h2.astype(h2_ref.dtype)
    xo_ref[...] = xn


def _merge_call(x, ysb, ydf, gmix, wg, bg, wsb, wdf, wo, gffn, w_router, *, tm):
    n, d = x.shape
    router = w_router is not None
    row = lambda w: pl.BlockSpec((tm, w), lambda i: (i, 0))
    const = lambda a: pl.BlockSpec(a.shape, lambda i: (0, 0))
    args = [x, ysb, ydf, gmix.reshape(1, d), wg, bg.reshape(1, -1), wsb, wdf, wo,
            gffn.reshape(1, d)]
    in_specs = [row(d), row(ysb.shape[1]), row(ydf.shape[1])] + [const(a) for a in args[3:]]
    out_specs = [row(d), row(d)]
    out_shape = [jax.ShapeDtypeStruct((n, d), F32), jax.ShapeDtypeStruct((n, d), BF16)]
    if router:
        assert d == TOKEN_TILE_ROWS * LANES
        out_specs[1] = pl.BlockSpec((tm * TOKEN_TILE_ROWS, LANES), lambda i: (i, 0))
        out_shape[1] = jax.ShapeDtypeStruct((n * TOKEN_TILE_ROWS, LANES), F32)
        wr = jnp.zeros((d, LANES), BF16).at[:, :w_router.shape[1]].set(w_router.astype(BF16))
        args.append(wr)
        in_specs.append(const(wr))
        out_specs.append(row(LANES))
        out_shape.append(jax.ShapeDtypeStruct((n, LANES), F32))
    return pl.pallas_call(
        functools.partial(_merge_kernel, router=router),
        grid=(n // tm,),
        in_specs=in_specs,
        out_specs=out_specs,
        out_shape=out_shape,
        compiler_params=_params(("parallel",)),
        name="merge_router" if router else "merge",
    )(*args)


def _swiglu_kernel(x_ref, h_ref, wg_ref, wu_ref, wd_ref, o_ref, acc_ref):
    f = pl.program_id(1)

    @pl.when(f == 0)
    def _():
        acc_ref[...] = jnp.zeros_like(acc_ref)

    h = h_ref[...]
    g = _dot(h, wg_ref[...])
    u = _dot(h, wu_ref[...])
    act = (g * jax.nn.sigmoid(g)) * u
    acc_ref[...] += _dot(act.astype(BF16), wd_ref[...])

    @pl.when(f == pl.num_programs(1) - 1)
    def _():
        o_ref[...] = x_ref[...] + acc_ref[...]


def _swiglu_call(x, h, w_gate_up, w_down, *, tm, nf):
    n, d = x.shape
    d_ff = w_down.shape[0]
    tf = d_ff // nf
    assert tf * nf == d_ff and tf % LANES == 0
    row = pl.BlockSpec((tm, d), lambda i, f: (i, 0))
    return pl.pallas_call(
        _swiglu_kernel,
        grid=(n // tm, nf),
        in_specs=[row, row,
                  pl.BlockSpec((d, tf), lambda i, f: (0, f)),
                  pl.BlockSpec((d, tf), lambda i, f: (0, f + nf)),
                  pl.BlockSpec((tf, d), lambda i, f: (f, 0))],
        out_specs=row,
        out_shape=jax.ShapeDtypeStruct((n, d), F32),
        scratch_shapes=[pltpu.VMEM((tm, d), F32)],
        compiler_params=_params(("parallel", "arbitrary")),
        name="swiglu",
    )(x, h, w_gate_up, w_gate_up, w_down)


def _expert_kernel(be_ref, tok_ref, dst_ref, meta_ref, h_hbm, sw_ref, w1_ref, w3_ref, w2_ref,
                   y_hbm, xbuf, xb_ref, acc_ref, ybuf, zrow_ref, gsem, ssem, *, n_spare):
    del be_ref
    i = pl.program_id(0)
    f = pl.program_id(1)
    nf = pl.num_programs(1)
    nused = meta_ref[0]
    rows = xb_ref.shape[0]

    def tile_of(row):
        return pl.ds(pl.multiple_of(row * TOKEN_TILE_ROWS, TOKEN_TILE_ROWS), TOKEN_TILE_ROWS)

    def gather_copy(blk, r):
        return pltpu.make_async_copy(h_hbm.at[tile_of(tok_ref[blk * rows + r])],
                                     xbuf.at[blk % 2, tile_of(r)], gsem.at[blk % 2])

    def scatter_copy(blk, r):
        return pltpu.make_async_copy(ybuf.at[(blk + 2) % 2, tile_of(r)],
                                     y_hbm.at[tile_of(dst_ref[(blk + 1) * rows + r])],
                                     ssem.at[(blk + 2) % 2])

    def for_rows(lo, n, fn):
        def body(r, carry):
            fn(lo + r)
            return carry
        lax.fori_loop(0, n, body, 0, unroll=8)

    @pl.when(jnp.logical_and(i == 0, f == 0))
    def _():
        ybuf[1] = jnp.zeros(ybuf.shape[1:], ybuf.dtype)
        for_rows(0, rows, lambda r: gather_copy(0, r).start())

    @pl.when(jnp.logical_and(f == 0, i <= nused))
    def _():
        for_rows(0, rows, lambda r: gather_copy(i, r).wait())

    @pl.when(jnp.logical_and(f == 0, i < nused))
    def _():
        xb_ref[...] = _load_token_tiles(xbuf.at[i % 2], rows).astype(BF16)
        acc_ref[...] = jnp.zeros_like(acc_ref)

    chunk = rows // nf

    @pl.when(i < nused)
    def _():
        for r in range(chunk):
            gather_copy(i + 1, f * chunk + r).start()
            scatter_copy(i - 1, f * chunk + r).start()
        xb = xb_ref[...]
        g = _dot(xb, w1_ref[...])
        u = _dot(xb, w3_ref[...])
        hid = (g * jax.nn.sigmoid(g)) * u
        acc_ref[...] += _dot(hid.astype(BF16), w2_ref[...])

    @pl.when(i == nused)
    def _():
        for_rows(f * chunk, chunk, lambda r: scatter_copy(i - 1, r).start())

    @pl.when(f == nf - 1)
    def _():
        @pl.when(jnp.logical_and(i < nused, i >= 1))
        def _():
            for_rows(0, rows, lambda r: scatter_copy(i - 2, r).wait())

        @pl.when(i < nused)
        def _():
            _store_token_tiles(ybuf.at[i % 2], acc_ref[...] * sw_ref[...])

        @pl.when(i == nused)
        def _():
            @pl.when(i >= 1)
            def _():
                for_rows(0, rows, lambda r: scatter_copy(i - 2, r).wait())
            for_rows(0, rows, lambda r: scatter_copy(i - 1, r).wait())

            zrow_ref[...] = jnp.zeros_like(zrow_ref)
            n_pad = meta_ref[1]
            spare_lo = y_hbm.shape[0] // TOKEN_TILE_ROWS - rows - n_spare

            def zero_copy(r):
                return pltpu.make_async_copy(zrow_ref, y_hbm.at[tile_of(spare_lo + r)],
                                             ssem.at[0])

            def fill(r, carry):
                zero_copy(r).start()
                return carry

            def drain(r, carry):
                zero_copy(r).wait()
                return carry

            lax.fori_loop(n_pad, n_spare, fill, 0)
            lax.fori_loop(n_pad, n_spare, drain, 0)


def _expert_call(block_e, slot_tok, dst_all, meta, h, slot_w, w1, w3, w2, *, n_spare, nf):
    n_slots = slot_tok.shape[0]
    n_tok = h.shape[0] // TOKEN_TILE_ROWS
    n_out = n_tok * TOP_K + n_spare + MOE_BLOCK
    n_blocks = n_slots // MOE_BLOCK
    d = w1.shape[1]
    d_ff = w1.shape[2]
    tf = d_ff // nf
    tile_rows = MOE_BLOCK * TOKEN_TILE_ROWS
    assert tf * nf == d_ff and tf % LANES == 0 and MOE_BLOCK % nf == 0
    assert d == TOKEN_TILE_ROWS * LANES and h.shape[1] == LANES
    def tile(i, f, meta):
        return jnp.where(i < meta[0], f, nf - 1)

    grid_spec = pltpu.PrefetchScalarGridSpec(
        num_scalar_prefetch=4,
        grid=(n_blocks, nf),
        in_specs=[pl.BlockSpec(memory_space=pl.ANY),
                  pl.BlockSpec((MOE_BLOCK, 1), lambda i, f, be, *_: (i, 0)),
                  pl.BlockSpec((None, d, tf), lambda i, f, be, tok, dst, meta:
                               (be[i], 0, tile(i, f, meta))),
                  pl.BlockSpec((None, d, tf), lambda i, f, be, tok, dst, meta:
                               (be[i], 0, tile(i, f, meta))),
                  pl.BlockSpec((None, tf, d), lambda i, f, be, tok, dst, meta:
                               (be[i], tile(i, f, meta), 0))],
        out_specs=pl.BlockSpec(memory_space=pl.ANY),
        scratch_shapes=[pltpu.VMEM((2, tile_rows, LANES), F32),
                        pltpu.VMEM((MOE_BLOCK, d), BF16),
                        pltpu.VMEM((MOE_BLOCK, d), F32),
                        pltpu.VMEM((2, tile_rows, LANES), F32),
                        pltpu.VMEM((TOKEN_TILE_ROWS, LANES), F32),
                        pltpu.SemaphoreType.DMA((2,)),
                        pltpu.SemaphoreType.DMA((2,))],
    )
    return pl.pallas_call(
        functools.partial(_expert_kernel, n_spare=n_spare),
        grid_spec=grid_spec,
        out_shape=jax.ShapeDtypeStruct((n_out * TOKEN_TILE_ROWS, LANES), F32),
        compiler_params=_params(("arbitrary", "arbitrary")),
        name="moe_experts",
    )(block_e, slot_tok, dst_all, meta, h, slot_w, w1, w3, w2)


def _combine_kernel(x_ref, y0_ref, y1_ref, o_ref):
    tm = x_ref.shape[0]
    o_ref[...] = x_ref[...] + (_load_token_tiles(y0_ref, tm) + _load_token_tiles(y1_ref, tm))


def _combine_call(x, y, *, tm):
    n, d = x.shape
    steps = n // tm
    y_rows = tm * TOKEN_TILE_ROWS
    return pl.pallas_call(
        _combine_kernel,
        grid=(steps,),
        in_specs=[pl.BlockSpec((tm, d), lambda i: (i, 0)),
                  pl.BlockSpec((y_rows, LANES), lambda i: (i, 0)),
                  pl.BlockSpec((y_rows, LANES), lambda i: (i + steps, 0))],
        out_specs=pl.BlockSpec((tm, d), lambda i: (i, 0)),
        out_shape=jax.ShapeDtypeStruct((n, d), F32),
        compiler_params=_params(("parallel",)),
        name="moe_combine",
    )(x, y, y)


def _route(logits):
    n_tok = logits.shape[0]
    n_assign = n_tok * TOP_K
    n_slots = -(-n_assign // MOE_BLOCK) * MOE_BLOCK + N_EXPERTS * MOE_BLOCK
    n_blocks = n_slots // MOE_BLOCK
    top_logit, top_idx = lax.top_k(logits, TOP_K)
    top_w = jax.nn.softmax(top_logit, axis=-1)
    onehot = jnp.sum(jax.nn.one_hot(top_idx, N_EXPERTS, dtype=jnp.int32), axis=1)
    csum = jnp.cumsum(onehot, axis=0)
    counts = csum[-1]
    padded = (counts + MOE_BLOCK - 1) // MOE_BLOCK * MOE_BLOCK
    padded_end = jnp.cumsum(padded)
    padded_start = padded_end - padded
    rank = jnp.take_along_axis(csum - onehot, top_idx, axis=1)
    slots = (padded_start[top_idx] + rank).astype(jnp.int32)
    dst = (jnp.arange(TOP_K, dtype=jnp.int32)[None, :] * n_tok
           + jnp.arange(n_tok, dtype=jnp.int32)[:, None])
    slot_dst = jnp.full((n_slots,), -1, jnp.int32).at[slots.reshape(-1)].set(dst.reshape(-1))
    real = slot_dst >= 0
    n_spare = n_slots - n_assign
    pad_rank = jnp.cumsum(jnp.logical_not(real).astype(jnp.int32)) - 1
    slot_tok = jnp.where(real, slot_dst % n_tok, 0)
    w_kmajor = top_w.T.reshape(-1)
    slot_w = jnp.where(real, w_kmajor[jnp.maximum(slot_dst, 0)], 0.0)
    slot_dst = jnp.where(real, slot_dst, n_assign + pad_rank)
    virtual = n_assign + n_spare + jnp.arange(MOE_BLOCK, dtype=jnp.int32)
    dst_all = jnp.concatenate([virtual, slot_dst])
    block_e = jnp.minimum(
        jnp.searchsorted(padded_end, jnp.arange(n_blocks) * MOE_BLOCK, side='right'),
        N_EXPERTS - 1).astype(jnp.int32)
    meta = jnp.stack([padded_end[-1] // MOE_BLOCK, padded_end[-1] - n_assign]).astype(jnp.int32)
    return block_e, slot_tok, dst_all, meta, slot_w.reshape(n_slots, 1), n_spare


def kernel(x, norm_mix_g, w_in, diff_q_norm_g, diff_k_norm_g, diff_lambda, diff_subln_g,
           w_gate, b_gate, w_branch_sb, w_branch_diff, w_out, norm_ffn_g,
           ffn_w_gate_up, ffn_w_down, moe_w_router, moe_w1, moe_w3, moe_w2):
    batch, seq, d = x.shape
    n = batch * seq
    depth = w_in.shape[0]
    tm = min(512, n)
    tq = min(256, seq)
    xf = x.reshape(n, d).astype(F32)
    bf = lambda w: w.astype(BF16)
    for i in range(depth):
        sbk, dfk, sbq, sbv, dfq, dfv = _proj_call(
            xf, norm_mix_g[i], w_in[i], diff_q_norm_g[i], diff_k_norm_g[i], tm=tm)
        y_sb = _sb_call(sbq, sbk, sbv, batch=batch, tq=tq, n_heads=4)
        lambda_init = 0.8 - 0.6 * math.exp(-0.3 * i)
        y_df = _diff_call(dfq, dfk, dfv, diff_lambda[i], diff_subln_g[i],
                          batch=batch, tq=min(512, seq), tk=min(1024, seq),
                          lambda_init=lambda_init)
        j = i // 2
        dense = i % 2 == 0
        outs = _merge_call(xf, y_sb, y_df, norm_mix_g[i], bf(w_gate[i]), b_gate[i],
                           bf(w_branch_sb[i]), bf(w_branch_diff[i]), bf(w_out[i]),
                           norm_ffn_g[i], None if dense else moe_w_router[j], tm=tm)
        if dense:
            xf, h2 = outs
            xf = _swiglu_call(xf, h2, bf(ffn_w_gate_up[j]), bf(ffn_w_down[j]), tm=tm, nf=2)
        else:
            xf, h2, logits = outs
            block_e, slot_tok, dst_all, meta, slot_w, n_spare = _route(logits[:, :N_EXPERTS])
            y = _expert_call(block_e, slot_tok, dst_all, meta, h2, slot_w,
                             bf(moe_w1[j]), bf(moe_w3[j]), bf(moe_w2[j]), n_spare=n_spare, nf=2)
            xf = _combine_call(xf, y, tm=tm)
    return xf.reshape(batch, seq, d).astype(x.dtype)
```

```python
import functools
import math
from typing import NamedTuple

import jax
import jax.numpy as jnp
from jax import lax
from jax.experimental import pallas as pl
from jax.experimental.pallas import tpu as pltpu

F32 = jnp.float32
BF16 = jnp.bfloat16

HEAD_DIM = 64
NORM_EPS = 1e-6
N_EXPERTS = 8
TOP_K = 2
MOE_BLOCK = 512
LANES = 128
TOKEN_TILE_ROWS = 8
VMEM_LIMIT = 56 * 1024 * 1024

EXP_UNDERFLOW = -104.0
EXP2_UNDERFLOW = 152.0
MASKED = -1e30
LOG2E = math.log2(math.e)
V_ROWS = LANES + 16


def _dot(a, b):
    return jnp.dot(a, b, preferred_element_type=F32)


def _dot_nt(a, b):
    return lax.dot_general(a, b, (((1,), (1,)), ((), ())), preferred_element_type=F32)


def _rms(x, g):
    ms = jnp.mean(x * x, axis=-1, keepdims=True)
    return (x * lax.rsqrt(ms + NORM_EPS)) * g


def _split(x):
    hi = x.astype(BF16)
    return hi, (x - hi.astype(F32)).astype(BF16)


def _store_token_tiles(ref, x):
    rows = x.shape[0]
    for j in range(TOKEN_TILE_ROWS):
        ref[pl.ds(j, rows, stride=TOKEN_TILE_ROWS), :] = x[:, j * LANES:(j + 1) * LANES]


def _load_token_tiles(ref, rows):
    return jnp.concatenate([ref[pl.ds(j, rows, stride=TOKEN_TILE_ROWS), :]
                            for j in range(TOKEN_TILE_ROWS)], axis=1)


def _params(sem, vmem=VMEM_LIMIT):
    return pltpu.CompilerParams(dimension_semantics=sem, vmem_limit_bytes=vmem)


def _proj_kernel(x_ref, g_ref, wk_ref, wt_ref, bd_ref, gq_ref, gk_ref,
                 sbk_ref, dfk_ref, sbq_ref, sbv_ref, dfq_ref, dfv_ref, *, width):
    scale = HEAD_DIM ** -0.5
    h = _rms(x_ref[...], g_ref[...]).astype(BF16)
    bd = bd_ref[...]

    def feature_major(j):
        return _dot_nt(wt_ref[j * width:(j + 1) * width, :], h)

    sbk_ref[...] = _dot(h, wk_ref[:, :width]).astype(BF16)
    k = _dot(h, wk_ref[:, width:])
    hi, lo = _split(k * k)
    ss = _dot(hi, bd) + _dot(lo, bd)
    dfk_ref[...] = ((k * lax.rsqrt(ss * (1.0 / HEAD_DIM) + NORM_EPS)) * gk_ref[...]).astype(BF16)

    sbq_ref[...] = (feature_major(0) * scale).astype(BF16)
    sbv_ref[...] = feature_major(1).astype(BF16)
    q = feature_major(2)
    hi, lo = _split(q * q)
    ss = _dot(bd, hi) + _dot(bd, lo)
    q = (q * lax.rsqrt(ss * (1.0 / HEAD_DIM) + NORM_EPS)) * gq_ref[...]
    dfq_ref[...] = (q * (scale * LOG2E)).astype(BF16)
    v = feature_major(3).astype(BF16)
    for hd in range(width // LANES):
        dfv_ref[hd * V_ROWS:hd * V_ROWS + LANES, :] = v[hd * LANES:(hd + 1) * LANES, :]
        dfv_ref[hd * V_ROWS + LANES:(hd + 1) * V_ROWS, :] = jnp.ones(
            (V_ROWS - LANES, v.shape[1]), BF16)


def _proj_call(x, g, w_in, gq, gk, *, tm):
    n, d = x.shape
    width = w_in.shape[1] // 6
    cols = lambda j: w_in[:, j * width:(j + 1) * width]
    wk = jnp.concatenate([cols(1), cols(4)], axis=1).astype(BF16)
    wt = jnp.concatenate([cols(0), cols(2), cols(3), cols(5)], axis=1).T.astype(BF16)
    grp = lax.broadcasted_iota(jnp.int32, (width, width), 0) // HEAD_DIM
    bd = (grp == grp.T).astype(BF16)
    const = lambda a: pl.BlockSpec(a.shape, lambda i: (0, 0))
    tok_major = pl.BlockSpec((tm, width), lambda i: (i, 0))
    feat_major = pl.BlockSpec((width, tm), lambda i: (0, i))
    v_rows = width // LANES * V_ROWS
    args = [x, g.reshape(1, d), wk, wt, bd,
            jnp.tile(gq, width // HEAD_DIM).reshape(width, 1),
            jnp.tile(gk, width // HEAD_DIM).reshape(1, width)]
    return pl.pallas_call(
        functools.partial(_proj_kernel, width=width),
        grid=(n // tm,),
        in_specs=[pl.BlockSpec((tm, d), lambda i: (i, 0))] + [const(a) for a in args[1:]],
        out_specs=[tok_major] * 2 + [feat_major] * 3 + [pl.BlockSpec((v_rows, tm), lambda i: (0, i))],
        out_shape=[jax.ShapeDtypeStruct((n, width), BF16)] * 2
        + [jax.ShapeDtypeStruct((width, n), BF16)] * 3 + [jax.ShapeDtypeStruct((v_rows, n), BF16)],
        compiler_params=_params(("parallel",)),
        name="proj",
    )(*args)


def _sb_kernel(q_ref, k_ref, v_ref, o_ref, acc_ref, mask_ref, later_ref, *scratch, tq, n_heads):
    qi = pl.program_id(2)
    l_refs, row_refs, lb_refs, hi_refs, lo_refs = (
        scratch[j * n_heads:(j + 1) * n_heads] for j in range(5))

    @pl.when(qi == 0)
    def _():
        key = lax.broadcasted_iota(jnp.int32, (tq, tq), 0)
        other = lax.broadcasted_iota(jnp.int32, (tq, tq), 1)
        mask_ref[0] = jnp.where(key < other, 1.0, 0.0).astype(F32)
        mask_ref[1] = jnp.ones((tq, tq), F32)
        later_ref[...] = jnp.where(other > key, 1.0, 0.0).astype(BF16)

    feat = lax.broadcasted_iota(jnp.int32, q_ref.shape, 0)
    q = q_ref[...]
    q_heads = [jnp.where((feat >= hh * HEAD_DIM) & (feat < (hh + 1) * HEAD_DIM), q,
                         jnp.zeros_like(q)) for hh in range(n_heads)]

    acc_ref[...] = jnp.zeros_like(acc_ref)
    for hh in range(n_heads):
        l_refs[hh][...] = jnp.zeros_like(l_refs[hh])

    def logits(hh, kb):
        start = pl.multiple_of(kb * tq, tq)
        z = _dot(k_ref[pl.ds(start, tq), :], q_heads[hh])
        sp = jnp.maximum(z, 0.0) + jnp.log(1.0 + jnp.exp(-jnp.abs(z)))
        log_keep = -sp * mask_ref[jnp.where(kb == qi, 0, 1)]
        hi, lo = _split(log_keep)
        hi_refs[hh][...] = hi
        lo_refs[hh][...] = lo
        lb_refs[hh][...] = z - sp
        row_refs[hh][...] = log_keep[:1, :]

    def weights(hh, kb):
        feats = slice(hh * HEAD_DIM, (hh + 1) * HEAD_DIM)
        start = pl.multiple_of(kb * tq, tq)
        later_keys = later_ref[...]
        later = _dot(later_keys, hi_refs[hh][...]) + _dot(later_keys, lo_refs[hh][...])
        carry_l = l_refs[hh][...]
        w = jnp.exp(lb_refs[hh][...] + later + carry_l) * mask_ref[jnp.where(kb == qi, 0, 1)]
        acc_ref[feats, :] += _dot(v_ref[feats, pl.ds(start, tq)], w.astype(BF16))
        l_new = carry_l + later[:1, :] + row_refs[hh][...]
        l_refs[hh][...] = l_new
        return jnp.max(l_new)

    logits(0, qi)

    def cond(c):
        it, lm = c
        return jnp.logical_and(it <= qi, lm > EXP_UNDERFLOW)

    def body(c):
        it, _ = c
        kb = qi - it
        lm = None
        for hh in range(n_heads):
            if hh + 1 < n_heads:
                logits(hh + 1, kb)
            else:
                logits(0, jnp.maximum(kb - 1, 0))
            head_max = weights(hh, kb)
            lm = head_max if lm is None else jnp.maximum(lm, head_max)
        return it + 1, lm

    lax.while_loop(cond, body, (jnp.int32(0), jnp.float32(0.0)))
    o_ref[...] = acc_ref[...].T.astype(o_ref.dtype)


def _sb_call(q, k, v, *, batch, tq, n_heads):
    n, width = k.shape
    seq = n // batch
    nq = seq // tq
    fw = n_heads * HEAD_DIM
    return pl.pallas_call(
        functools.partial(_sb_kernel, tq=tq, n_heads=n_heads),
        grid=(batch, width // fw, nq),
        in_specs=[pl.BlockSpec((fw, tq), lambda b, h, i: (h, b * nq + i)),
                  pl.BlockSpec((seq, fw), lambda b, h, i: (b, h)),
                  pl.BlockSpec((fw, seq), lambda b, h, i: (h, b))],
        out_specs=pl.BlockSpec((tq, fw), lambda b, h, i: (b * nq + i, h)),
        out_shape=jax.ShapeDtypeStruct((n, width), BF16),
        scratch_shapes=[pltpu.VMEM((fw, tq), F32), pltpu.VMEM((2, tq, tq), F32),
                        pltpu.VMEM((tq, tq), BF16)]
        + [pltpu.VMEM((1, tq), F32)] * (2 * n_heads) + [pltpu.VMEM((tq, tq), F32)] * n_heads
        + [pltpu.VMEM((tq, tq), BF16)] * (2 * n_heads),
        compiler_params=_params(("parallel", "parallel", "arbitrary")),
        name="sb_attn",
    )(q, k, v)


def _diff_kernel(slopes_ref, lam_ref, g_ref, q_ref, k_ref, v_ref, o_ref, bias_ref,
                 s0_ref, s1_ref, mb0_ref, mb1_ref, m0_ref, m1_ref, acc0_ref, acc1_ref, kmax_ref,
                 *, tq, tk, lambda_init):
    h = pl.program_id(1)
    qi = pl.program_id(2)
    slope = slopes_ref[0, h]
    inv_slope = slopes_ref[1, h]
    nsub = tk // tq
    s_refs, mb_refs = (s0_ref, s1_ref), (mb0_ref, mb1_ref)
    m_refs, acc_refs = (m0_ref, m1_ref), (acc0_ref, acc1_ref)

    @pl.when(qi == 0)
    def _():
        key = lax.broadcasted_iota(jnp.int32, (tk, tq), 0)
        query = lax.broadcasted_iota(jnp.int32, (tk, tq), 1)
        key_ahead = key - query
        rel_bias = slope * key_ahead.astype(F32)
        bias_ref[nsub] = rel_bias
        for r in range(nsub):
            bias_ref[r] = jnp.where(key_ahead <= r * tq, rel_bias, MASKED)
        kf = k_ref[...].astype(F32)
        kmax_ref[0] = jnp.max(jnp.sum(kf * kf, axis=1, keepdims=True))

    feat = lax.broadcasted_iota(jnp.int32, (LANES, tq), 0)
    q = q_ref[...]
    q_maps = [jnp.where((feat >= c * HEAD_DIM) & (feat < (c + 1) * HEAD_DIM), q, jnp.zeros_like(q))
              for c in range(2)]

    for c in range(2):
        m_refs[c][...] = jnp.full_like(m_refs[c], MASKED)
        acc_refs[c][...] = jnp.zeros_like(acc_refs[c])

    last = (qi * tq) // tk

    qf = q.astype(F32)
    qk = jnp.max(jnp.sqrt(jnp.sum(qf * qf, axis=0, keepdims=True) * kmax_ref[0]))
    skip_dist = (EXP2_UNDERFLOW + 2.01 * qk) * inv_slope
    first = jnp.clip(((qi * tq).astype(F32) - skip_dist) * (1.0 / tk), 0.0, last.astype(F32))
    first = first.astype(jnp.int32)

    def offset(kb):
        return -slope * (qi * tq - kb * tk).astype(F32)

    half = tk // 2

    def scores(c, kb):
        sel = jnp.where(kb == last, qi % nsub, nsub)
        mb = None
        for part in range(2):
            rows = pl.ds(part * half, half)
            start = pl.multiple_of(kb * tk + part * half, half)
            s = _dot(k_ref[pl.ds(start, half), :], q_maps[c]) + bias_ref[sel, rows, :]
            s_refs[c][rows, :] = s
            part_max = jnp.max(s, axis=0, keepdims=True)
            mb = part_max if mb is None else jnp.maximum(mb, part_max)
        mb_refs[c][...] = mb + offset(kb)

    def accumulate(c, kb):
        m_old = m_refs[c][...]
        m_new = jnp.maximum(m_old, mb_refs[c][...])
        alpha = jnp.exp2(m_old - m_new)
        shift = m_new - offset(kb)
        pv = None
        for part in range(2):
            start = pl.multiple_of(kb * tk + part * half, half)
            p = jnp.exp2(s_refs[c][pl.ds(part * half, half), :] - shift)
            part_pv = _dot(v_ref[:, pl.ds(start, half)], p.astype(BF16))
            pv = part_pv if pv is None else pv + part_pv
        acc_refs[c][...] = alpha * acc_refs[c][...] + pv
        m_refs[c][...] = m_new

    scores(0, first)

    def step(kb):
        scores(1, kb)
        accumulate(0, kb)
        scores(0, jnp.minimum(kb + 1, last))
        accumulate(1, kb)

    def pair(j, carry):
        step(first + 2 * j)
        step(first + 2 * j + 1)
        return carry

    n_blocks = last + 1 - first
    lax.fori_loop(0, n_blocks // 2, pair, 0)

    @pl.when(n_blocks % 2 == 1)
    def _():
        step(last)

    lp = lam_ref[...]
    lam = (jnp.exp(jnp.sum(lp[0:1] * lp[1:2], axis=-1, keepdims=True))
           - jnp.exp(jnp.sum(lp[2:3] * lp[3:4], axis=-1, keepdims=True)) + lambda_init)
    a0, a1 = acc0_ref[...], acc1_ref[...]
    o = (a0[:LANES] * (1.0 / a0[LANES:LANES + 1])
         - lam * (a1[:LANES] * (1.0 / a1[LANES:LANES + 1])))
    ms = jnp.mean(o * o, axis=0, keepdims=True)
    o = ((o * lax.rsqrt(ms + NORM_EPS)) * g_ref[...]) * (1.0 - lambda_init)
    o_ref[...] = o.T.astype(o_ref.dtype)


def _diff_call(q, k, v, lam_params, subln_g, *, batch, tq, tk, lambda_init):
    n, width = k.shape
    heads = width // LANES
    assert tk % tq == 0
    slopes = [2.0 ** (-8.0 * (h + 1) / heads) * LOG2E for h in range(heads)]
    slopes = jnp.asarray([slopes, [1.0 / s for s in slopes]], dtype=F32)
    seq = n // batch
    nq = seq // tq
    return pl.pallas_call(
        functools.partial(_diff_kernel, tq=tq, tk=tk, lambda_init=lambda_init),
        grid=(batch, heads, nq),
        in_specs=[pl.BlockSpec(memory_space=pltpu.SMEM),
                  pl.BlockSpec(lam_params.shape, lambda b, h, i: (0, 0)),
                  pl.BlockSpec((LANES, 1), lambda b, h, i: (0, 0)),
                  pl.BlockSpec((LANES, tq), lambda b, h, i: (h, b * nq + i)),
                  pl.BlockSpec((seq, LANES), lambda b, h, i: (b, h)),
                  pl.BlockSpec((V_ROWS, seq), lambda b, h, i: (h, b))],
        out_specs=pl.BlockSpec((tq, LANES), lambda b, h, i: (b * nq + i, h)),
        out_shape=jax.ShapeDtypeStruct((n, width), BF16),
        scratch_shapes=[pltpu.VMEM((tk // tq + 1, tk, tq), F32)]
        + [pltpu.VMEM((tk, tq), F32)] * 2 + [pltpu.VMEM((1, tq), F32)] * 4
        + [pltpu.VMEM((V_ROWS, tq), F32)] * 2 + [pltpu.SMEM((1,), F32)],
        compiler_params=_params(("parallel", "parallel", "arbitrary")),
        name="diff_attn",
    )(slopes, lam_params.astype(F32), subln_g.reshape(LANES, 1).astype(F32), q, k, v)


def _merge_kernel(*refs, router):
    (x_ref, ysb_ref, ydf_ref, gmix_ref, wg_ref, bg_ref, wsb_ref, wdf_ref, wo_ref,
     gffn_ref) = refs[:10]
    rest = refs[10:]
    x = x_ref[...]
    d = x.shape[-1]
    h = _rms(x, gmix_ref[...]).astype(BF16)
    gates = jax.nn.sigmoid(_dot(h, wg_ref[...]) + bg_ref[...])
    merged = (gates[:, :d] * _dot(ysb_ref[...], wsb_ref[...])
              + gates[:, d:] * _dot(ydf_ref[...], wdf_ref[...]))
    xn = x + _dot(merged.astype(BF16), wo_ref[...])
    h2 = _rms(xn, gffn_ref[...])
    if router:
        wr_ref, xo_ref, h2_ref, lg_ref = rest
        lg_ref[...] = _dot(h2.astype(BF16), wr_ref[...])
        _store_token_tiles(h2_ref, h2)
    else:
        xo_ref, h2_ref = rest
        h2_ref[...] = h2.astype(h2_ref.dtype)
    xo_ref[...] = xn


def _merge_call(x, ysb, ydf, gmix, wg, bg, wsb, wdf, wo, gffn, w_router, *, tm):
    n, d = x.shape
    router = w_router is not None
    row = lambda w: pl.BlockSpec((tm, w), lambda i: (i, 0))
    const = lambda a: pl.BlockSpec(a.shape, lambda i: (0, 0))
    args = [x, ysb, ydf, gmix.reshape(1, d), wg, bg.reshape(1, -1), wsb, wdf, wo,
            gffn.reshape(1, d)]
    in_specs = [row(d), row(ysb.shape[1]), row(ydf.shape[1])] + [const(a) for a in args[3:]]
    out_specs = [row(d), row(d)]
    out_shape = [jax.ShapeDtypeStruct((n, d), F32), jax.ShapeDtypeStruct((n, d), BF16)]
    if router:
        assert d == TOKEN_TILE_ROWS * LANES
        out_specs[1] = pl.BlockSpec((tm * TOKEN_TILE_ROWS, LANES), lambda i: (i, 0))
        out_shape[1] = jax.ShapeDtypeStruct((n * TOKEN_TILE_ROWS, LANES), F32)
        wr = jnp.zeros((d, LANES), BF16).at[:, :w_router.shape[1]].set(w_router.astype(BF16))
        args.append(wr)
        in_specs.append(const(wr))
        out_specs.append(row(LANES))
        out_shape.append(jax.ShapeDtypeStruct((n, LANES), F32))
    return pl.pallas_call(
        functools.partial(_merge_kernel, router=router),
        grid=(n // tm,),
        in_specs=in_specs,
        out_specs=out_specs,
        out_shape=out_shape,
        compiler_params=_params(("parallel",)),
        name="merge_router" if router else "merge",
    )(*args)


def _swiglu_kernel(x_ref, h_ref, wg_ref, wu_ref, wd_ref, o_ref, acc_ref):
    f = pl.program_id(1)

    @pl.when(f == 0)
    def _():
        acc_ref[...] = jnp.zeros_like(acc_ref)

    h = h_ref[...]
    g = _dot(h, wg_ref[...])
    u = _dot(h, wu_ref[...])
    act = (g * jax.nn.sigmoid(g)) * u
    acc_ref[...] += _dot(act.astype(BF16), wd_ref[...])

    @pl.when(f == pl.num_programs(1) - 1)
    def _():
        o_ref[...] = x_ref[...] + acc_ref[...]


def _swiglu_call(x, h, w_gate_up, w_down, *, tm, nf):
    n, d = x.shape
    d_ff = w_down.shape[0]
    tf = d_ff // nf
    assert tf * nf == d_ff and tf % LANES == 0
    row = pl.BlockSpec((tm, d), lambda i, f: (i, 0))
    return pl.pallas_call(
        _swiglu_kernel,
        grid=(n // tm, nf),
        in_specs=[row, row,
                  pl.BlockSpec((d, tf), lambda i, f: (0, f)),
                  pl.BlockSpec((d, tf), lambda i, f: (0, f + nf)),
                  pl.BlockSpec((tf, d), lambda i, f: (f, 0))],
        out_specs=row,
        out_shape=jax.ShapeDtypeStruct((n, d), F32),
        scratch_shapes=[pltpu.VMEM((tm, d), F32)],
        compiler_params=_params(("parallel", "arbitrary")),
        name="swiglu",
    )(x, h, w_gate_up, w_gate_up, w_down)


def _expert_kernel(be_ref, tok_ref, dst_ref, meta_ref, h_hbm, sw_ref, w1_ref, w3_ref, w2_ref,
                   y_hbm, xbuf, xb_ref, acc_ref, ybuf, zrow_ref, gsem, ssem, *, n_spare):
    del be_ref
    i = pl.program_id(0)
    f = pl.program_id(1)
    nf = pl.num_programs(1)
    nused = meta_ref[0]
    rows = xb_ref.shape[0]

    def tile_of(row):
        return pl.ds(pl.multiple_of(row * TOKEN_TILE_ROWS, TOKEN_TILE_ROWS), TOKEN_TILE_ROWS)

    def gather_copy(blk, r):
        return pltpu.make_async_copy(h_hbm.at[tile_of(tok_ref[blk * rows + r])],
                                     xbuf.at[blk % 2, tile_of(r)], gsem.at[blk % 2])

    def scatter_copy(blk, r):
        return pltpu.make_async_copy(ybuf.at[(blk + 2) % 2, tile_of(r)],
                                     y_hbm.at[tile_of(dst_ref[(blk + 1) * rows + r])],
                                     ssem.at[(blk + 2) % 2])

    def for_rows(lo, n, fn):
        def body(r, carry):
            fn(lo + r)
            return carry
        lax.fori_loop(0, n, body, 0, unroll=8)

    @pl.when(jnp.logical_and(i == 0, f == 0))
    def _():
        ybuf[1] = jnp.zeros(ybuf.shape[1:], ybuf.dtype)
        for_rows(0, rows, lambda r: gather_copy(0, r).start())

    @pl.when(jnp.logical_and(f == 0, i <= nused))
    def _():
        for_rows(0, rows, lambda r: gather_copy(i, r).wait())

    @pl.when(jnp.logical_and(f == 0, i < nused))
    def _():
        xb_ref[...] = _load_token_tiles(xbuf.at[i % 2], rows).astype(BF16)
        acc_ref[...] = jnp.zeros_like(acc_ref)

    chunk = rows // nf

    @pl.when(i < nused)
    def _():
        for r in range(chunk):
            gather_copy(i + 1, f * chunk + r).start()
            scatter_copy(i - 1, f * chunk + r).start()
        xb = xb_ref[...]
        g = _dot(xb, w1_ref[...])
        u = _dot(xb, w3_ref[...])
        hid = (g * jax.nn.sigmoid(g)) * u
        acc_ref[...] += _dot(hid.astype(BF16), w2_ref[...])

    @pl.when(i == nused)
    def _():
        for_rows(f * chunk, chunk, lambda r: scatter_copy(i - 1, r).start())

    @pl.when(f == nf - 1)
    def _():
        @pl.when(jnp.logical_and(i < nused, i >= 1))
        def _():
            for_rows(0, rows, lambda r: scatter_copy(i - 2, r).wait())

        @pl.when(i < nused)
        def _():
            _store_token_tiles(ybuf.at[i % 2], acc_ref[...] * sw_ref[...])

        @pl.when(i == nused)
        def _():
            @pl.when(i >= 1)
            def _():
                for_rows(0, rows, lambda r: scatter_copy(i - 2, r).wait())
            for_rows(0, rows, lambda r: scatter_copy(i - 1, r).wait())

            zrow_ref[...] = jnp.zeros_like(zrow_ref)
            n_pad = meta_ref[1]
            spare_lo = y_hbm.shape[0] // TOKEN_TILE_ROWS - rows - n_spare

            def zero_copy(r):
                return pltpu.make_async_copy(zrow_ref, y_hbm.at[tile_of(spare_lo + r)],
                                             ssem.at[0])

            def fill(r, carry):
                zero_copy(r).start()
                return carry

            def drain(r, carry):
                zero_copy(r).wait()
                return carry

            lax.fori_loop(n_pad, n_spare, fill, 0)
            lax.fori_loop(n_pad, n_spare, drain, 0)


def _expert_call(block_e, slot_tok, dst_all, meta, h, slot_w, w1, w3, w2, *, n_spare, nf):
    n_slots = slot_tok.shape[0]
    n_tok = h.shape[0] // TOKEN_TILE_ROWS
    n_out = n_tok * TOP_K + n_spare + MOE_BLOCK
    n_blocks = n_slots // MOE_BLOCK
    d = w1.shape[1]
    d_ff = w1.shape[2]
    tf = d_ff // nf
    tile_rows = MOE_BLOCK * TOKEN_TILE_ROWS
    assert tf * nf == d_ff and tf % LANES == 0 and MOE_BLOCK % nf == 0
    assert d == TOKEN_TILE_ROWS * LANES and h.shape[1] == LANES
    def tile(i, f, meta):
        return jnp.where(i < meta[0], f, nf - 1)

    grid_spec = pltpu.PrefetchScalarGridSpec(
        num_scalar_prefetch=4,
        grid=(n_blocks, nf),
        in_specs=[pl.BlockSpec(memory_space=pl.ANY),
                  pl.BlockSpec((MOE_BLOCK, 1), lambda i, f, be, *_: (i, 0)),
                  pl.BlockSpec((None, d, tf), lambda i, f, be, tok, dst, meta:
                               (be[i], 0, tile(i, f, meta))),
                  pl.BlockSpec((None, d, tf), lambda i, f, be, tok, dst, meta:
                               (be[i], 0, tile(i, f, meta))),
                  pl.BlockSpec((None, tf, d), lambda i, f, be, tok, dst, meta:
                               (be[i], tile(i, f, meta), 0))],
        out_specs=pl.BlockSpec(memory_space=pl.ANY),
        scratch_shapes=[pltpu.VMEM((2, tile_rows, LANES), F32),
                        pltpu.VMEM((MOE_BLOCK, d), BF16),
                        pltpu.VMEM((MOE_BLOCK, d), F32),
                        pltpu.VMEM((2, tile_rows, LANES), F32),
                        pltpu.VMEM((TOKEN_TILE_ROWS, LANES), F32),
                        pltpu.SemaphoreType.DMA((2,)),
                        pltpu.SemaphoreType.DMA((2,))],
    )
    return pl.pallas_call(
        functools.partial(_expert_kernel, n_spare=n_spare),
        grid_spec=grid_spec,
        out_shape=jax.ShapeDtypeStruct((n_out * TOKEN_TILE_ROWS, LANES), F32),
        compiler_params=_params(("arbitrary", "arbitrary")),
        name="moe_experts",
    )(block_e, slot_tok, dst_all, meta, h, slot_w, w1, w3, w2)


def _combine_kernel(x_ref, y0_ref, y1_ref, o_ref):
    tm = x_ref.shape[0]
    o_ref[...] = x_ref[...] + (_load_token_tiles(y0_ref, tm) + _load_token_tiles(y1_ref, tm))


def _combine_call(x, y, *, tm):
    n, d = x.shape
    steps = n // tm
    y_rows = tm * TOKEN_TILE_ROWS
    return pl.pallas_call(
        _combine_kernel,
        grid=(steps,),
        in_specs=[pl.BlockSpec((tm, d), lambda i: (i, 0)),
                  pl.BlockSpec((y_rows, LANES), lambda i: (i, 0)),
                  pl.BlockSpec((y_rows, LANES), lambda i: (i + steps, 0))],
        out_specs=pl.BlockSpec((tm, d), lambda i: (i, 0)),
        out_shape=jax.ShapeDtypeStruct((n, d), F32),
        compiler_params=_params(("parallel",)),
        name="moe_combine",
    )(x, y, y)


def _route(logits):
    n_tok = logits.shape[0]
    n_assign = n_tok * TOP_K
    n_slots = -(-n_assign // MOE_BLOCK) * MOE_BLOCK + N_EXPERTS * MOE_BLOCK
    n_blocks = n_slots // MOE_BLOCK
    top_logit, top_idx = lax.top_k(logits, TOP_K)
    top_w = jax.nn.softmax(top_logit, axis=-1)
    onehot = jnp.sum(jax.nn.one_hot(top_idx, N_EXPERTS, dtype=jnp.int32), axis=1)
    csum = jnp.cumsum(onehot, axis=0)
    counts = csum[-1]
    padded = (counts + MOE_BLOCK - 1) // MOE_BLOCK * MOE_BLOCK
    padded_end = jnp.cumsum(padded)
    padded_start = padded_end - padded
    rank = jnp.take_along_axis(csum - onehot, top_idx, axis=1)
    slots = (padded_start[top_idx] + rank).astype(jnp.int32)
    dst = (jnp.arange(TOP_K, dtype=jnp.int32)[None, :] * n_tok
           + jnp.arange(n_tok, dtype=jnp.int32)[:, None])
    slot_dst = jnp.full((n_slots,), -1, jnp.int32).at[slots.reshape(-1)].set(dst.reshape(-1))
    real = slot_dst >= 0
    n_spare = n_slots - n_assign
    pad_rank = jnp.cumsum(jnp.logical_not(real).astype(jnp.int32)) - 1
    slot_tok = jnp.where(real, slot_dst % n_tok, 0)
    w_kmajor = top_w.T.reshape(-1)
    slot_w = jnp.where(real, w_kmajor[jnp.maximum(slot_dst, 0)], 0.0)
    slot_dst = jnp.where(real, slot_dst, n_assign + pad_rank)
    virtual = n_assign + n_spare + jnp.arange(MOE_BLOCK, dtype=jnp.int32)
    dst_all = jnp.concatenate([virtual, slot_dst])
    block_e = jnp.minimum(
        jnp.searchsorted(padded_end, jnp.arange(n_blocks) * MOE_BLOCK, side='right'),
        N_EXPERTS - 1).astype(jnp.int32)
    meta = jnp.stack([padded_end[-1] // MOE_BLOCK, padded_end[-1] - n_assign]).astype(jnp.int32)
    return block_e, slot_tok, dst_all, meta, slot_w.reshape(n_slots, 1), n_spare


class _Tiles(NamedTuple):
    rows: int
    sb_q: int
    sb_heads: int
    diff_q: int
    diff_k: int
    ff_splits: int


def _tiles(n, seq, branch_width):
    return _Tiles(rows=min(512, n), sb_q=min(256, seq), sb_heads=branch_width // HEAD_DIM,
                  diff_q=min(512, seq), diff_k=min(1024, seq), ff_splits=2)


def kernel(x, norm_mix_g, w_in, diff_q_norm_g, diff_k_norm_g, diff_lambda, diff_subln_g,
           w_gate, b_gate, w_branch_sb, w_branch_diff, w_out, norm_ffn_g,
           ffn_w_gate_up, ffn_w_down, moe_w_router, moe_w1, moe_w3, moe_w2):
    batch, seq, d = x.shape
    n = batch * seq
    depth = w_in.shape[0]
    t = _tiles(n, seq, w_in.shape[2] // 6)
    xf = x.reshape(n, d).astype(F32)
    bf = lambda w: w.astype(BF16)
    for i in range(depth):
        sbk, dfk, sbq, sbv, dfq, dfv = _proj_call(
            xf, norm_mix_g[i], w_in[i], diff_q_norm_g[i], diff_k_norm_g[i], tm=t.rows)
        y_sb = _sb_call(sbq, sbk, sbv, batch=batch, tq=t.sb_q, n_heads=t.sb_heads)
        lambda_init = 0.8 - 0.6 * math.exp(-0.3 * i)
        y_df = _diff_call(dfq, dfk, dfv, diff_lambda[i], diff_subln_g[i],
                          batch=batch, tq=t.diff_q, tk=t.diff_k, lambda_init=lambda_init)
        j = i // 2
        dense = i % 2 == 0
        outs = _merge_call(xf, y_sb, y_df, norm_mix_g[i], bf(w_gate[i]), b_gate[i],
                           bf(w_branch_sb[i]), bf(w_branch_diff[i]), bf(w_out[i]),
                           norm_ffn_g[i], None if dense else moe_w_router[j], tm=t.rows)
        if dense:
            xf, h2 = outs
            xf = _swiglu_call(xf, h2, bf(ffn_w_gate_up[j]), bf(ffn_w_down[j]),
                              tm=t.rows, nf=t.ff_splits)
        else:
            xf, h2, logits = outs
            block_e, slot_tok, dst_all, meta, slot_w, n_spare = _route(logits[:, :N_EXPERTS])
            y = _expert_call(block_e, slot_tok, dst_all, meta, h2, slot_w,
                             bf(moe_w1[j]), bf(moe_w3[j]), bf(moe_w2[j]),
                             n_spare=n_spare, nf=t.ff_splits)
            xf = _combine_call(xf, y, tm=t.rows)
    return xf.reshape(batch, seq, d).astype(x.dtype)
```

```python
import functools
import math
from typing import NamedTuple

import jax
import jax.numpy as jnp
from jax import lax
from jax.experimental import pallas as pl
from jax.experimental.pallas import tpu as pltpu

F32 = jnp.float32
BF16 = jnp.bfloat16

HEAD_DIM = 64
NORM_EPS = 1e-6
N_EXPERTS = 8
TOP_K = 2
MOE_BLOCK = 512
LANES = 128
TOKEN_TILE_ROWS = 8
VMEM_LIMIT = 56 * 1024 * 1024

EXP_UNDERFLOW = -104.0
EXP2_UNDERFLOW = 152.0
MASKED = -1e30
LOG2E = math.log2(math.e)
V_ROWS = LANES + 16


def _dot(a, b):
    return jnp.dot(a, b, preferred_element_type=F32)


def _dot_nt(a, b):
    return lax.dot_general(a, b, (((1,), (1,)), ((), ())), preferred_element_type=F32)


def _rms(x, g):
    ms = jnp.mean(x * x, axis=-1, keepdims=True)
    return (x * lax.rsqrt(ms + NORM_EPS)) * g


def _split(x):
    hi = x.astype(BF16)
    return hi, (x - hi.astype(F32)).astype(BF16)


def _store_token_tiles(ref, x):
    rows = x.shape[0]
    for j in range(TOKEN_TILE_ROWS):
        ref[pl.ds(j, rows, stride=TOKEN_TILE_ROWS), :] = x[:, j * LANES:(j + 1) * LANES]


def _load_token_tiles(ref, rows):
    return jnp.concatenate([ref[pl.ds(j, rows, stride=TOKEN_TILE_ROWS), :]
                            for j in range(TOKEN_TILE_ROWS)], axis=1)


def _params(sem, vmem=VMEM_LIMIT):
    return pltpu.CompilerParams(dimension_semantics=sem, vmem_limit_bytes=vmem)


def _proj_kernel(x_ref, g_ref, wk_ref, wt_ref, bd_ref, gq_ref, gk_ref,
                 sbk_ref, dfk_ref, sbq_ref, sbv_ref, dfq_ref, dfv_ref, *, width):
    scale = HEAD_DIM ** -0.5
    h = _rms(x_ref[...], g_ref[...]).astype(BF16)
    bd = bd_ref[...]

    def feature_major(j):
        return _dot_nt(wt_ref[j * width:(j + 1) * width, :], h)

    sbk_ref[...] = _dot(h, wk_ref[:, :width]).astype(BF16)
    k = _dot(h, wk_ref[:, width:])
    hi, lo = _split(k * k)
    ss = _dot(hi, bd) + _dot(lo, bd)
    dfk_ref[...] = ((k * lax.rsqrt(ss * (1.0 / HEAD_DIM) + NORM_EPS)) * gk_ref[...]).astype(BF16)

    sbq_ref[...] = (feature_major(0) * scale).astype(BF16)
    sbv_ref[...] = feature_major(1).astype(BF16)
    q = feature_major(2)
    hi, lo = _split(q * q)
    ss = _dot(bd, hi) + _dot(bd, lo)
    q = (q * lax.rsqrt(ss * (1.0 / HEAD_DIM) + NORM_EPS)) * gq_ref[...]
    dfq_ref[...] = (q * (scale * LOG2E)).astype(BF16)
    v = feature_major(3).astype(BF16)
    for hd in range(width // LANES):
        dfv_ref[hd * V_ROWS:hd * V_ROWS + LANES, :] = v[hd * LANES:(hd + 1) * LANES, :]
        dfv_ref[hd * V_ROWS + LANES:(hd + 1) * V_ROWS, :] = jnp.ones(
            (V_ROWS - LANES, v.shape[1]), BF16)


def _proj_call(x, g, w_in, gq, gk, *, tm):
    n, d = x.shape
    width = w_in.shape[1] // 6
    cols = lambda j: w_in[:, j * width:(j + 1) * width]
    wk = jnp.concatenate([cols(1), cols(4)], axis=1).astype(BF16)
    wt = jnp.concatenate([cols(0), cols(2), cols(3), cols(5)], axis=1).T.astype(BF16)
    grp = lax.broadcasted_iota(jnp.int32, (width, width), 0) // HEAD_DIM
    bd = (grp == grp.T).astype(BF16)
    const = lambda a: pl.BlockSpec(a.shape, lambda i: (0, 0))
    tok_major = pl.BlockSpec((tm, width), lambda i: (i, 0))
    feat_major = pl.BlockSpec((width, tm), lambda i: (0, i))
    v_rows = width // LANES * V_ROWS
    args = [x, g.reshape(1, d), wk, wt, bd,
            jnp.tile(gq, width // HEAD_DIM).reshape(width, 1),
            jnp.tile(gk, width // HEAD_DIM).reshape(1, width)]
    return pl.pallas_call(
        functools.partial(_proj_kernel, width=width),
        grid=(n // tm,),
        in_specs=[pl.BlockSpec((tm, d), lambda i: (i, 0))] + [const(a) for a in args[1:]],
        out_specs=[tok_major] * 2 + [feat_major] * 3 + [pl.BlockSpec((v_rows, tm), lambda i: (0, i))],
        out_shape=[jax.ShapeDtypeStruct((n, width), BF16)] * 2
        + [jax.ShapeDtypeStruct((width, n), BF16)] * 3 + [jax.ShapeDtypeStruct((v_rows, n), BF16)],
        compiler_params=_params(("parallel",)),
        name="proj",
    )(*args)


def _sb_kernel(q_ref, k_ref, v_ref, o_ref, acc_ref, mask_ref, later_ref, *scratch, tq, n_heads):
    qi = pl.program_id(2)
    l_refs, row_refs, lb_refs, hi_refs, lo_refs = (
        scratch[j * n_heads:(j + 1) * n_heads] for j in range(5))

    @pl.when(qi == 0)
    def _():
        key = lax.broadcasted_iota(jnp.int32, (tq, tq), 0)
        other = lax.broadcasted_iota(jnp.int32, (tq, tq), 1)
        mask_ref[0] = jnp.where(key < other, 1.0, 0.0).astype(F32)
        mask_ref[1] = jnp.ones((tq, tq), F32)
        later_ref[...] = jnp.where(other > key, 1.0, 0.0).astype(BF16)

    feat = lax.broadcasted_iota(jnp.int32, q_ref.shape, 0)
    q = q_ref[...]
    q_heads = [jnp.where((feat >= hh * HEAD_DIM) & (feat < (hh + 1) * HEAD_DIM), q,
                         jnp.zeros_like(q)) for hh in range(n_heads)]

    acc_ref[...] = jnp.zeros_like(acc_ref)
    for hh in range(n_heads):
        l_refs[hh][...] = jnp.zeros_like(l_refs[hh])

    def logits(hh, kb):
        start = pl.multiple_of(kb * tq, tq)
        z = _dot(k_ref[pl.ds(start, tq), :], q_heads[hh])
        sp = jnp.maximum(z, 0.0) + jnp.log(1.0 + jnp.exp(-jnp.abs(z)))
        log_keep = -sp * mask_ref[jnp.where(kb == qi, 0, 1)]
        hi, lo = _split(log_keep)
        hi_refs[hh][...] = hi
        lo_refs[hh][...] = lo
        lb_refs[hh][...] = z - sp
        row_refs[hh][...] = log_keep[:1, :]

    def weights(hh, kb):
        feats = slice(hh * HEAD_DIM, (hh + 1) * HEAD_DIM)
        start = pl.multiple_of(kb * tq, tq)
        later_keys = later_ref[...]
        later = _dot(later_keys, hi_refs[hh][...]) + _dot(later_keys, lo_refs[hh][...])
        carry_l = l_refs[hh][...]
        w = jnp.exp(lb_refs[hh][...] + later + carry_l) * mask_ref[jnp.where(kb == qi, 0, 1)]
        acc_ref[feats, :] += _dot(v_ref[feats, pl.ds(start, tq)], w.astype(BF16))
        l_new = carry_l + later[:1, :] + row_refs[hh][...]
        l_refs[hh][...] = l_new
        return jnp.max(l_new)

    logits(0, qi)

    def cond(c):
        it, lm = c
        return jnp.logical_and(it <= qi, lm > EXP_UNDERFLOW)

    def body(c):
        it, _ = c
        kb = qi - it
        lm = None
        for hh in range(n_heads):
            if hh + 1 < n_heads:
                logits(hh + 1, kb)
            else:
                logits(0, jnp.maximum(kb - 1, 0))
            head_max = weights(hh, kb)
            lm = head_max if lm is None else jnp.maximum(lm, head_max)
        return it + 1, lm

    lax.while_loop(cond, body, (jnp.int32(0), jnp.float32(0.0)))
    o_ref[...] = acc_ref[...].T.astype(o_ref.dtype)


def _sb_call(q, k, v, *, batch, tq, n_heads):
    n, width = k.shape
    seq = n // batch
    nq = seq // tq
    fw = n_heads * HEAD_DIM
    return pl.pallas_call(
        functools.partial(_sb_kernel, tq=tq, n_heads=n_heads),
        grid=(batch, width // fw, nq),
        in_specs=[pl.BlockSpec((fw, tq), lambda b, h, i: (h, b * nq + i)),
                  pl.BlockSpec((seq, fw), lambda b, h, i: (b, h)),
                  pl.BlockSpec((fw, seq), lambda b, h, i: (h, b))],
        out_specs=pl.BlockSpec((tq, fw), lambda b, h, i: (b * nq + i, h)),
        out_shape=jax.ShapeDtypeStruct((n, width), BF16),
        scratch_shapes=[pltpu.VMEM((fw, tq), F32), pltpu.VMEM((2, tq, tq), F32),
                        pltpu.VMEM((tq, tq), BF16)]
        + [pltpu.VMEM((1, tq), F32)] * (2 * n_heads) + [pltpu.VMEM((tq, tq), F32)] * n_heads
        + [pltpu.VMEM((tq, tq), BF16)] * (2 * n_heads),
        compiler_params=_params(("parallel", "parallel", "arbitrary")),
        name="sb_attn",
    )(q, k, v)


def _diff_kernel(slopes_ref, lam_ref, g_ref, q_ref, k_ref, v_ref, o_ref, bias_ref,
                 s0_ref, s1_ref, mb0_ref, mb1_ref, m0_ref, m1_ref, acc0_ref, acc1_ref, kmax_ref,
                 *, tq, tk, lambda_init):
    h = pl.program_id(1)
    qi = pl.program_id(2)
    slope = slopes_ref[0, h]
    inv_slope = slopes_ref[1, h]
    nsub = tk // tq
    s_refs, mb_refs = (s0_ref, s1_ref), (mb0_ref, mb1_ref)
    m_refs, acc_refs = (m0_ref, m1_ref), (acc0_ref, acc1_ref)

    @pl.when(qi == 0)
    def _():
        key = lax.broadcasted_iota(jnp.int32, (tk, tq), 0)
        query = lax.broadcasted_iota(jnp.int32, (tk, tq), 1)
        key_ahead = key - query
        rel_bias = slope * key_ahead.astype(F32)
        bias_ref[nsub] = rel_bias
        for r in range(nsub):
            bias_ref[r] = jnp.where(key_ahead <= r * tq, rel_bias, MASKED)
        kf = k_ref[...].astype(F32)
        kmax_ref[0] = jnp.max(jnp.sum(kf * kf, axis=1, keepdims=True))

    feat = lax.broadcasted_iota(jnp.int32, (LANES, tq), 0)
    q = q_ref[...]
    q_maps = [jnp.where((feat >= c * HEAD_DIM) & (feat < (c + 1) * HEAD_DIM), q, jnp.zeros_like(q))
              for c in range(2)]

    for c in range(2):
        m_refs[c][...] = jnp.full_like(m_refs[c], MASKED)
        acc_refs[c][...] = jnp.zeros_like(acc_refs[c])

    last = (qi * tq) // tk

    qf = q.astype(F32)
    qk = jnp.max(jnp.sqrt(jnp.sum(qf * qf, axis=0, keepdims=True) * kmax_ref[0]))
    skip_dist = (EXP2_UNDERFLOW + 2.01 * qk) * inv_slope
    first = jnp.clip(((qi * tq).astype(F32) - skip_dist) * (1.0 / tk), 0.0, last.astype(F32))
    first = first.astype(jnp.int32)

    def offset(kb):
        return -slope * (qi * tq - kb * tk).astype(F32)

    half = tk // 2

    def scores(c, kb):
        sel = jnp.where(kb == last, qi % nsub, nsub)
        mb = None
        for part in range(2):
            rows = pl.ds(part * half, half)
            start = pl.multiple_of(kb * tk + part * half, half)
            s = _dot(k_ref[pl.ds(start, half), :], q_maps[c]) + bias_ref[sel, rows, :]
            s_refs[c][rows, :] = s
            part_max = jnp.max(s, axis=0, keepdims=True)
            mb = part_max if mb is None else jnp.maximum(mb, part_max)
        mb_refs[c][...] = mb + offset(kb)

    def accumulate(c, kb):
        m_old = m_refs[c][...]
        m_new = jnp.maximum(m_old, mb_refs[c][...])
        alpha = jnp.exp2(m_old - m_new)
        shift = m_new - offset(kb)
        pv = None
        for part in range(2):
            start = pl.multiple_of(kb * tk + part * half, half)
            p = jnp.exp2(s_refs[c][pl.ds(part * half, half), :] - shift)
            part_pv = _dot(v_ref[:, pl.ds(start, half)], p.astype(BF16))
            pv = part_pv if pv is None else pv + part_pv
        acc_refs[c][...] = alpha * acc_refs[c][...] + pv
        m_refs[c][...] = m_new

    scores(0, first)

    def step(kb):
        scores(1, kb)
        accumulate(0, kb)
        scores(0, jnp.minimum(kb + 1, last))
        accumulate(1, kb)

    def pair(j, carry):
        step(first + 2 * j)
        step(first + 2 * j + 1)
        return carry

    n_blocks = last + 1 - first
    lax.fori_loop(0, n_blocks // 2, pair, 0)

    @pl.when(n_blocks % 2 == 1)
    def _():
        step(last)

    lp = lam_ref[...]
    lam = (jnp.exp(jnp.sum(lp[0:1] * lp[1:2], axis=-1, keepdims=True))
           - jnp.exp(jnp.sum(lp[2:3] * lp[3:4], axis=-1, keepdims=True)) + lambda_init)
    a0, a1 = acc0_ref[...], acc1_ref[...]
    o = (a0[:LANES] * (1.0 / a0[LANES:LANES + 1])
         - lam * (a1[:LANES] * (1.0 / a1[LANES:LANES + 1])))
    ms = jnp.mean(o * o, axis=0, keepdims=True)
    o = ((o * lax.rsqrt(ms + NORM_EPS)) * g_ref[...]) * (1.0 - lambda_init)
    o_ref[...] = o.T.astype(o_ref.dtype)


def _diff_call(q, k, v, lam_params, subln_g, *, batch, tq, tk, lambda_init):
    n, width = k.shape
    heads = width // LANES
    assert tk % tq == 0
    slopes = [2.0 ** (-8.0 * (h + 1) / heads) * LOG2E for h in range(heads)]
    slopes = jnp.asarray([slopes, [1.0 / s for s in slopes]], dtype=F32)
    seq = n // batch
    nq = seq // tq
    return pl.pallas_call(
        functools.partial(_diff_kernel, tq=tq, tk=tk, lambda_init=lambda_init),
        grid=(batch, heads, nq),
        in_specs=[pl.BlockSpec(memory_space=pltpu.SMEM),
                  pl.BlockSpec(lam_params.shape, lambda b, h, i: (0, 0)),
                  pl.BlockSpec((LANES, 1), lambda b, h, i: (0, 0)),
                  pl.BlockSpec((LANES, tq), lambda b, h, i: (h, b * nq + i)),
                  pl.BlockSpec((seq, LANES), lambda b, h, i: (b, h)),
                  pl.BlockSpec((V_ROWS, seq), lambda b, h, i: (h, b))],
        out_specs=pl.BlockSpec((tq, LANES), lambda b, h, i: (b * nq + i, h)),
        out_shape=jax.ShapeDtypeStruct((n, width), BF16),
        scratch_shapes=[pltpu.VMEM((tk // tq + 1, tk, tq), F32)]
        + [pltpu.VMEM((tk, tq), F32)] * 2 + [pltpu.VMEM((1, tq), F32)] * 4
        + [pltpu.VMEM((V_ROWS, tq), F32)] * 2 + [pltpu.SMEM((1,), F32)],
        compiler_params=_params(("parallel", "parallel", "arbitrary")),
        name="diff_attn",
    )(slopes, lam_params.astype(F32), subln_g.reshape(LANES, 1).astype(F32), q, k, v)


def _merge_kernel(*refs, router):
    (x_ref, ysb_ref, ydf_ref, gmix_ref, wg_ref, bg_ref, wsb_ref, wdf_ref, wo_ref,
     gffn_ref) = refs[:10]
    rest = refs[10:]
    x = x_ref[...]
    d = x.shape[-1]
    h = _rms(x, gmix_ref[...]).astype(BF16)
    gates = jax.nn.sigmoid(_dot(h, wg_ref[...]) + bg_ref[...])
    merged = (gates[:, :d] * _dot(ysb_ref[...], wsb_ref[...])
              + gates[:, d:] * _dot(ydf_ref[...], wdf_ref[...]))
    xn = x + _dot(merged.astype(BF16), wo_ref[...])
    h2 = _rms(xn, gffn_ref[...])
    if router:
        wr_ref, xo_ref, h2_ref, lg_ref = rest
        lg_ref[...] = _dot(h2.astype(BF16), wr_ref[...])
        _store_token_tiles(h2_ref, h2)
    else:
        xo_ref, h2_ref = rest
        h2_ref[...] = h2.astype(h2_ref.dtype)
    xo_ref[...] = xn


def _merge_call(x, ysb, ydf, gmix, wg, bg, wsb, wdf, wo, gffn, w_router, *, tm):
    n, d = x.shape
    router = w_router is not None
    row = lambda w: pl.BlockSpec((tm, w), lambda i: (i, 0))
    const = lambda a: pl.BlockSpec(a.shape, lambda i: (0, 0))
    args = [x, ysb, ydf, gmix.reshape(1, d), wg, bg.reshape(1, -1), wsb, wdf, wo,
            gffn.reshape(1, d)]
    in_specs = [row(d), row(ysb.shape[1]), row(ydf.shape[1])] + [const(a) for a in args[3:]]
    out_specs = [row(d), row(d)]
    out_shape = [jax.ShapeDtypeStruct((n, d), F32), jax.ShapeDtypeStruct((n, d), BF16)]
    if router:
        assert d == TOKEN_TILE_ROWS * LANES
        out_specs[1] = pl.BlockSpec((tm * TOKEN_TILE_ROWS, LANES), lambda i: (i, 0))
        out_shape[1] = jax.ShapeDtypeStruct((n * TOKEN_TILE_ROWS, LANES), F32)
        wr = jnp.zeros((d, LANES), BF16).at[:, :w_router.shape[1]].set(w_router.astype(BF16))
        args.append(wr)
        in_specs.append(const(wr))
        out_specs.append(row(LANES))
        out_shape.append(jax.ShapeDtypeStruct((n, LANES), F32))
    return pl.pallas_call(
        functools.partial(_merge_kernel, router=router),
        grid=(n // tm,),
        in_specs=in_specs,
        out_specs=out_specs,
        out_shape=out_shape,
        compiler_params=_params(("parallel",)),
        name="merge_router" if router else "merge",
    )(*args)


def _swiglu_kernel(x_ref, h_ref, wg_ref, wu_ref, wd_ref, o_ref, acc_ref):
    f = pl.program_id(1)

    @pl.when(f == 0)
    def _():
        acc_ref[...] = jnp.zeros_like(acc_ref)

    h = h_ref[...]
    g = _dot(h, wg_ref[...])
    u = _dot(h, wu_ref[...])
    act = (g * jax.nn.sigmoid(g)) * u
    acc_ref[...] += _dot(act.astype(BF16), wd_ref[...])

    @pl.when(f == pl.num_programs(1) - 1)
    def _():
        o_ref[...] = x_ref[...] + acc_ref[...]


def _swiglu_call(x, h, w_gate_up, w_down, *, tm, nf):
    n, d = x.shape
    d_ff = w_down.shape[0]
    tf = d_ff // nf
    assert tf * nf == d_ff and tf % LANES == 0
    row = pl.BlockSpec((tm, d), lambda i, f: (i, 0))
    return pl.pallas_call(
        _swiglu_kernel,
        grid=(n // tm, nf),
        in_specs=[row, row,
                  pl.BlockSpec((d, tf), lambda i, f: (0, f)),
                  pl.BlockSpec((d, tf), lambda i, f: (0, f + nf)),
                  pl.BlockSpec((tf, d), lambda i, f: (f, 0))],
        out_specs=row,
        out_shape=jax.ShapeDtypeStruct((n, d), F32),
        scratch_shapes=[pltpu.VMEM((tm, d), F32)],
        compiler_params=_params(("parallel", "arbitrary")),
        name="swiglu",
    )(x, h, w_gate_up, w_gate_up, w_down)


def _expert_kernel(be_ref, tok_ref, dst_ref, meta_ref, h_hbm, sw_ref, w1_ref, w3_ref, w2_ref,
                   y_hbm, xbuf, xb_ref, acc_ref, ybuf, zrow_ref, gsem, ssem, *, n_spare):
    del be_ref
    i = pl.program_id(0)
    f = pl.program_id(1)
    nf = pl.num_programs(1)
    nused = meta_ref[0]
    rows = xb_ref.shape[0]

    def tile_of(row):
        return pl.ds(pl.multiple_of(row * TOKEN_TILE_ROWS, TOKEN_TILE_ROWS), TOKEN_TILE_ROWS)

    def gather_copy(blk, r):
        return pltpu.make_async_copy(h_hbm.at[tile_of(tok_ref[blk * rows + r])],
                                     xbuf.at[blk % 2, tile_of(r)], gsem.at[blk % 2])

    def scatter_copy(blk, r):
        return pltpu.make_async_copy(ybuf.at[(blk + 2) % 2, tile_of(r)],
                                     y_hbm.at[tile_of(dst_ref[(blk + 1) * rows + r])],
                                     ssem.at[(blk + 2) % 2])

    def for_rows(lo, n, fn):
        def body(r, carry):
            fn(lo + r)
            return carry
        lax.fori_loop(0, n, body, 0, unroll=8)

    @pl.when(jnp.logical_and(i == 0, f == 0))
    def _():
        ybuf[1] = jnp.zeros(ybuf.shape[1:], ybuf.dtype)
        for_rows(0, rows, lambda r: gather_copy(0, r).start())

    @pl.when(jnp.logical_and(f == 0, i <= nused))
    def _():
        for_rows(0, rows, lambda r: gather_copy(i, r).wait())

    @pl.when(jnp.logical_and(f == 0, i < nused))
    def _():
        xb_ref[...] = _load_token_tiles(xbuf.at[i % 2], rows).astype(BF16)
        acc_ref[...] = jnp.zeros_like(acc_ref)

    chunk = rows // nf

    @pl.when(i < nused)
    def _():
        for r in range(chunk):
            gather_copy(i + 1, f * chunk + r).start()
            scatter_copy(i - 1, f * chunk + r).start()
        xb = xb_ref[...]
        g = _dot(xb, w1_ref[...])
        u = _dot(xb, w3_ref[...])
        hid = (g * jax.nn.sigmoid(g)) * u
        acc_ref[...] += _dot(hid.astype(BF16), w2_ref[...])

    @pl.when(i == nused)
    def _():
        for_rows(f * chunk, chunk, lambda r: scatter_copy(i - 1, r).start())

    @pl.when(f == nf - 1)
    def _():
        @pl.when(jnp.logical_and(i < nused, i >= 1))
        def _():
            for_rows(0, rows, lambda r: scatter_copy(i - 2, r).wait())

        @pl.when(i < nused)
        def _():
            _store_token_tiles(ybuf.at[i % 2], acc_ref[...] * sw_ref[...])

        @pl.when(i == nused)
        def _():
            @pl.when(i >= 1)
            def _():
                for_rows(0, rows, lambda r: scatter_copy(i - 2, r).wait())
            for_rows(0, rows, lambda r: scatter_copy(i - 1, r).wait())

            zrow_ref[...] = jnp.zeros_like(zrow_ref)
            n_pad = meta_ref[1]
            spare_lo = y_hbm.shape[0] // TOKEN_TILE_ROWS - rows - n_spare

            def zero_copy(r):
                return pltpu.make_async_copy(zrow_ref, y_hbm.at[tile_of(spare_lo + r)],
                                             ssem.at[0])

            def fill(r, carry):
                zero_copy(r).start()
                return carry

            def drain(r, carry):
                zero_copy(r).wait()
                return carry

            lax.fori_loop(n_pad, n_spare, fill, 0)
            lax.fori_loop(n_pad, n_spare, drain, 0)


def _expert_call(block_e, slot_tok, dst_all, meta, h, slot_w, w1, w3, w2, *, n_spare, nf):
    n_slots = slot_tok.shape[0]
    n_tok = h.shape[0] // TOKEN_TILE_ROWS
    n_out = n_tok * TOP_K + n_spare + MOE_BLOCK
    n_blocks = n_slots // MOE_BLOCK
    d = w1.shape[1]
    d_ff = w1.shape[2]
    tf = d_ff // nf
    tile_rows = MOE_BLOCK * TOKEN_TILE_ROWS
    assert tf * nf == d_ff and tf % LANES == 0 and MOE_BLOCK % nf == 0
    assert d == TOKEN_TILE_ROWS * LANES and h.shape[1] == LANES
    def tile(i, f, meta):
        return jnp.where(i < meta[0], f, nf - 1)

    grid_spec = pltpu.PrefetchScalarGridSpec(
        num_scalar_prefetch=4,
        grid=(n_blocks, nf),
        in_specs=[pl.BlockSpec(memory_space=pl.ANY),
                  pl.BlockSpec((MOE_BLOCK, 1), lambda i, f, be, *_: (i, 0)),
                  pl.BlockSpec((None, d, tf), lambda i, f, be, tok, dst, meta:
                               (be[i], 0, tile(i, f, meta))),
                  pl.BlockSpec((None, d, tf), lambda i, f, be, tok, dst, meta:
                               (be[i], 0, tile(i, f, meta))),
                  pl.BlockSpec((None, tf, d), lambda i, f, be, tok, dst, meta:
                               (be[i], tile(i, f, meta), 0))],
        out_specs=pl.BlockSpec(memory_space=pl.ANY),
        scratch_shapes=[pltpu.VMEM((2, tile_rows, LANES), F32),
                        pltpu.VMEM((MOE_BLOCK, d), BF16),
                        pltpu.VMEM((MOE_BLOCK, d), F32),
                        pltpu.VMEM((2, tile_rows, LANES), F32),
                        pltpu.VMEM((TOKEN_TILE_ROWS, LANES), F32),
                        pltpu.SemaphoreType.DMA((2,)),
                        pltpu.SemaphoreType.DMA((2,))],
    )
    return pl.pallas_call(
        functools.partial(_expert_kernel, n_spare=n_spare),
        grid_spec=grid_spec,
        out_shape=jax.ShapeDtypeStruct((n_out * TOKEN_TILE_ROWS, LANES), F32),
        compiler_params=_params(("arbitrary", "arbitrary")),
        name="moe_experts",
    )(block_e, slot_tok, dst_all, meta, h, slot_w, w1, w3, w2)


def _combine_kernel(x_ref, y0_ref, y1_ref, o_ref):
    tm = x_ref.shape[0]
    o_ref[...] = x_ref[...] + (_load_token_tiles(y0_ref, tm) + _load_token_tiles(y1_ref, tm))


def _combine_call(x, y, *, tm):
    n, d = x.shape
    steps = n // tm
    y_rows = tm * TOKEN_TILE_ROWS
    return pl.pallas_call(
        _combine_kernel,
        grid=(steps,),
        in_specs=[pl.BlockSpec((tm, d), lambda i: (i, 0)),
                  pl.BlockSpec((y_rows, LANES), lambda i: (i, 0)),
                  pl.BlockSpec((y_rows, LANES), lambda i: (i + steps, 0))],
        out_specs=pl.BlockSpec((tm, d), lambda i: (i, 0)),
        out_shape=jax.ShapeDtypeStruct((n, d), F32),
        compiler_params=_params(("parallel",)),
        name="moe_combine",
    )(x, y, y)


def _route(logits):
    n_tok = logits.shape[0]
    n_assign = n_tok * TOP_K
    n_slots = -(-n_assign // MOE_BLOCK) * MOE_BLOCK + N_EXPERTS * MOE_BLOCK
    n_blocks = n_slots // MOE_BLOCK
    top_logit, top_idx = lax.top_k(logits, TOP_K)
    top_w = jax.nn.softmax(top_logit, axis=-1)
    onehot = jnp.sum(jax.nn.one_hot(top_idx, N_EXPERTS, dtype=jnp.int32), axis=1)
    csum = jnp.cumsum(onehot, axis=0)
    counts = csum[-1]
    padded = (counts + MOE_BLOCK - 1) // MOE_BLOCK * MOE_BLOCK
    padded_end = jnp.cumsum(padded)
    padded_start = padded_end - padded
    rank = jnp.take_along_axis(csum - onehot, top_idx, axis=1)
    slots = (padded_start[top_idx] + rank).astype(jnp.int32)
    dst = (jnp.arange(TOP_K, dtype=jnp.int32)[None, :] * n_tok
           + jnp.arange(n_tok, dtype=jnp.int32)[:, None])
    slot_dst = jnp.full((n_slots,), -1, jnp.int32).at[slots.reshape(-1)].set(dst.reshape(-1))
    real = slot_dst >= 0
    n_spare = n_slots - n_assign
    pad_rank = jnp.cumsum(jnp.logical_not(real).astype(jnp.int32)) - 1
    slot_tok = jnp.where(real, slot_dst % n_tok, 0)
    w_kmajor = top_w.T.reshape(-1)
    slot_w = jnp.where(real, w_kmajor[jnp.maximum(slot_dst, 0)], 0.0)
    slot_dst = jnp.where(real, slot_dst, n_assign + pad_rank)
    virtual = n_assign + n_spare + jnp.arange(MOE_BLOCK, dtype=jnp.int32)
    dst_all = jnp.concatenate([virtual, slot_dst])
    block_e = jnp.minimum(
        jnp.searchsorted(padded_end, jnp.arange(n_blocks) * MOE_BLOCK, side='right'),
        N_EXPERTS - 1).astype(jnp.int32)
    meta = jnp.stack([padded_end[-1] // MOE_BLOCK, padded_end[-1] - n_assign]).astype(jnp.int32)
    return block_e, slot_tok, dst_all, meta, slot_w.reshape(n_slots, 1), n_spare


class _Tiles(NamedTuple):
    rows: int
    sb_q: int
    sb_heads: int
    diff_q: int
    diff_k: int
    ff_splits: int


def _tiles(n, seq, branch_width):
    return _Tiles(rows=min(512, n), sb_q=min(256, seq), sb_heads=branch_width // HEAD_DIM,
                  diff_q=min(1024, seq), diff_k=min(1024, seq), ff_splits=2)


def kernel(x, norm_mix_g, w_in, diff_q_norm_g, diff_k_norm_g, diff_lambda, diff_subln_g,
           w_gate, b_gate, w_branch_sb, w_branch_diff, w_out, norm_ffn_g,
           ffn_w_gate_up, ffn_w_down, moe_w_router, moe_w1, moe_w3, moe_w2):
    batch, seq, d = x.shape
    n = batch * seq
    depth = w_in.shape[0]
    t = _tiles(n, seq, w_in.shape[2] // 6)
    xf = x.reshape(n, d).astype(F32)
    bf = lambda w: w.astype(BF16)
    for i in range(depth):
        sbk, dfk, sbq, sbv, dfq, dfv = _proj_call(
            xf, norm_mix_g[i], w_in[i], diff_q_norm_g[i], diff_k_norm_g[i], tm=t.rows)
        y_sb = _sb_call(sbq, sbk, sbv, batch=batch, tq=t.sb_q, n_heads=t.sb_heads)
        lambda_init = 0.8 - 0.6 * math.exp(-0.3 * i)
        y_df = _diff_call(dfq, dfk, dfv, diff_lambda[i], diff_subln_g[i],
                          batch=batch, tq=t.diff_q, tk=t.diff_k, lambda_init=lambda_init)
        j = i // 2
        dense = i % 2 == 0
        outs = _merge_call(xf, y_sb, y_df, norm_mix_g[i], bf(w_gate[i]), b_gate[i],
                           bf(w_branch_sb[i]), bf(w_branch_diff[i]), bf(w_out[i]),
                           norm_ffn_g[i], None if dense else moe_w_router[j], tm=t.rows)
        if dense:
            xf, h2 = outs
            xf = _swiglu_call(xf, h2, bf(ffn_w_gate_up[j]), bf(ffn_w_down[j]),
                              tm=t.rows, nf=t.ff_splits)
        else:
            xf, h2, logits = outs
            block_e, slot_tok, dst_all, meta, slot_w, n_spare = _route(logits[:, :N_EXPERTS])
            y = _expert_call(block_e, slot_tok, dst_all, meta, h2, slot_w,
                             bf(moe_w1[j]), bf(moe_w3[j]), bf(moe_w2[j]),
                             n_spare=n_spare, nf=t.ff_splits)
            xf = _combine_call(xf, y, tm=t.rows)
    return xf.reshape(batch, seq, d).astype(x.dtype)
```

```python
import functools
import math
from typing import NamedTuple

import jax
import jax.numpy as jnp
from jax import lax
from jax.experimental import pallas as pl
from jax.experimental.pallas import tpu as pltpu

F32 = jnp.float32
BF16 = jnp.bfloat16

HEAD_DIM = 64
NORM_EPS = 1e-6
N_EXPERTS = 8
TOP_K = 2
MOE_BLOCK = 512
LANES = 128
TOKEN_TILE_ROWS = 8
VMEM_LIMIT = 56 * 1024 * 1024

EXP_UNDERFLOW = -104.0
EXP2_UNDERFLOW = 152.0
MASKED = -1e30
LOG2E = math.log2(math.e)
V_ROWS = LANES + 16


def _dot(a, b):
    return jnp.dot(a, b, preferred_element_type=F32)


def _dot_nt(a, b):
    return lax.dot_general(a, b, (((1,), (1,)), ((), ())), preferred_element_type=F32)


def _rms(x, g):
    ms = jnp.mean(x * x, axis=-1, keepdims=True)
    return (x * lax.rsqrt(ms + NORM_EPS)) * g


def _split(x):
    hi = x.astype(BF16)
    return hi, (x - hi.astype(F32)).astype(BF16)


def _store_token_tiles(ref, x):
    rows = x.shape[0]
    for j in range(TOKEN_TILE_ROWS):
        ref[pl.ds(j, rows, stride=TOKEN_TILE_ROWS), :] = x[:, j * LANES:(j + 1) * LANES]


def _load_token_tiles(ref, rows):
    return jnp.concatenate([ref[pl.ds(j, rows, stride=TOKEN_TILE_ROWS), :]
                            for j in range(TOKEN_TILE_ROWS)], axis=1)


def _params(sem, vmem=VMEM_LIMIT):
    return pltpu.CompilerParams(dimension_semantics=sem, vmem_limit_bytes=vmem)


def _proj_kernel(x_ref, g_ref, wk_ref, wt_ref, bd_ref, gq_ref, gk_ref,
                 sbk_ref, dfk_ref, sbq_ref, sbv_ref, dfq_ref, dfv_ref, *, width):
    scale = HEAD_DIM ** -0.5
    h = _rms(x_ref[...], g_ref[...]).astype(BF16)
    bd = bd_ref[...]

    def feature_major(j):
        return _dot_nt(wt_ref[j * width:(j + 1) * width, :], h)

    sbk_ref[...] = _dot(h, wk_ref[:, :width]).astype(BF16)
    k = _dot(h, wk_ref[:, width:])
    hi, lo = _split(k * k)
    ss = _dot(hi, bd) + _dot(lo, bd)
    dfk_ref[...] = ((k * lax.rsqrt(ss * (1.0 / HEAD_DIM) + NORM_EPS)) * gk_ref[...]).astype(BF16)

    sbq_ref[...] = (feature_major(0) * scale).astype(BF16)
    sbv_ref[...] = feature_major(1).astype(BF16)
    q = feature_major(2)
    hi, lo = _split(q * q)
    ss = _dot(bd, hi) + _dot(bd, lo)
    q = (q * lax.rsqrt(ss * (1.0 / HEAD_DIM) + NORM_EPS)) * gq_ref[...]
    dfq_ref[...] = (q * (scale * LOG2E)).astype(BF16)
    v = feature_major(3).astype(BF16)
    for hd in range(width // LANES):
        dfv_ref[hd * V_ROWS:hd * V_ROWS + LANES, :] = v[hd * LANES:(hd + 1) * LANES, :]
        dfv_ref[hd * V_ROWS + LANES:(hd + 1) * V_ROWS, :] = jnp.ones(
            (V_ROWS - LANES, v.shape[1]), BF16)


def _proj_call(x, g, w_in, gq, gk, *, tm):
    n, d = x.shape
    width = w_in.shape[1] // 6
    cols = lambda j: w_in[:, j * width:(j + 1) * width]
    wk = jnp.concatenate([cols(1), cols(4)], axis=1).astype(BF16)
    wt = jnp.concatenate([cols(0), cols(2), cols(3), cols(5)], axis=1).T.astype(BF16)
    grp = lax.broadcasted_iota(jnp.int32, (width, width), 0) // HEAD_DIM
    bd = (grp == grp.T).astype(BF16)
    const = lambda a: pl.BlockSpec(a.shape, lambda i: (0, 0))
    tok_major = pl.BlockSpec((tm, width), lambda i: (i, 0))
    feat_major = pl.BlockSpec((width, tm), lambda i: (0, i))
    v_rows = width // LANES * V_ROWS
    args = [x, g.reshape(1, d), wk, wt, bd,
            jnp.tile(gq, width // HEAD_DIM).reshape(width, 1),
            jnp.tile(gk, width // HEAD_DIM).reshape(1, width)]
    return pl.pallas_call(
        functools.partial(_proj_kernel, width=width),
        grid=(n // tm,),
        in_specs=[pl.BlockSpec((tm, d), lambda i: (i, 0))] + [const(a) for a in args[1:]],
        out_specs=[tok_major] * 2 + [feat_major] * 3 + [pl.BlockSpec((v_rows, tm), lambda i: (0, i))],
        out_shape=[jax.ShapeDtypeStruct((n, width), BF16)] * 2
        + [jax.ShapeDtypeStruct((width, n), BF16)] * 3 + [jax.ShapeDtypeStruct((v_rows, n), BF16)],
        compiler_params=_params(("parallel",)),
        name="proj",
    )(*args)


def _sb_kernel(q_ref, k_ref, v_ref, o_ref, acc_ref, mask_ref, later_ref, *scratch, tq, n_heads):
    qi = pl.program_id(2)
    l_refs, row_refs, lb_refs, hi_refs, lo_refs = (
        scratch[j * n_heads:(j + 1) * n_heads] for j in range(5))

    @pl.when(qi == 0)
    def _():
        key = lax.broadcasted_iota(jnp.int32, (tq, tq), 0)
        other = lax.broadcasted_iota(jnp.int32, (tq, tq), 1)
        mask_ref[0] = jnp.where(key < other, 1.0, 0.0).astype(F32)
        mask_ref[1] = jnp.ones((tq, tq), F32)
        later_ref[...] = jnp.where(other > key, 1.0, 0.0).astype(BF16)

    feat = lax.broadcasted_iota(jnp.int32, q_ref.shape, 0)
    q = q_ref[...]
    q_heads = [jnp.where((feat >= hh * HEAD_DIM) & (feat < (hh + 1) * HEAD_DIM), q,
                         jnp.zeros_like(q)) for hh in range(n_heads)]

    acc_ref[...] = jnp.zeros_like(acc_ref)
    for hh in range(n_heads):
        l_refs[hh][...] = jnp.zeros_like(l_refs[hh])

    def logits(hh, kb):
        start = pl.multiple_of(kb * tq, tq)
        z = _dot(k_ref[pl.ds(start, tq), :], q_heads[hh])
        sp = jnp.maximum(z, 0.0) + jnp.log(1.0 + jnp.exp(-jnp.abs(z)))
        log_keep = -sp * mask_ref[jnp.where(kb == qi, 0, 1)]
        hi, lo = _split(log_keep)
        hi_refs[hh][...] = hi
        lo_refs[hh][...] = lo
        lb_refs[hh][...] = z - sp
        row_refs[hh][...] = log_keep[:1, :]

    def weights(hh, kb):
        feats = slice(hh * HEAD_DIM, (hh + 1) * HEAD_DIM)
        start = pl.multiple_of(kb * tq, tq)
        later_keys = later_ref[...]
        later = _dot(later_keys, hi_refs[hh][...]) + _dot(later_keys, lo_refs[hh][...])
        carry_l = l_refs[hh][...]
        w = jnp.exp(lb_refs[hh][...] + later + carry_l) * mask_ref[jnp.where(kb == qi, 0, 1)]
        acc_ref[feats, :] += _dot(v_ref[feats, pl.ds(start, tq)], w.astype(BF16))
        l_new = carry_l + later[:1, :] + row_refs[hh][...]
        l_refs[hh][...] = l_new
        return jnp.max(l_new)

    logits(0, qi)

    def cond(c):
        it, lm = c
        return jnp.logical_and(it <= qi, lm > EXP_UNDERFLOW)

    def body(c):
        it, _ = c
        kb = qi - it
        lm = None
        for hh in range(n_heads):
            if hh + 1 < n_heads:
                logits(hh + 1, kb)
            else:
                logits(0, jnp.maximum(kb - 1, 0))
            head_max = weights(hh, kb)
            lm = head_max if lm is None else jnp.maximum(lm, head_max)
        return it + 1, lm

    lax.while_loop(cond, body, (jnp.int32(0), jnp.float32(0.0)))
    o_ref[...] = acc_ref[...].T.astype(o_ref.dtype)


def _sb_call(q, k, v, *, batch, tq, n_heads):
    n, width = k.shape
    seq = n // batch
    nq = seq // tq
    fw = n_heads * HEAD_DIM
    return pl.pallas_call(
        functools.partial(_sb_kernel, tq=tq, n_heads=n_heads),
        grid=(batch, width // fw, nq),
        in_specs=[pl.BlockSpec((fw, tq), lambda b, h, i: (h, b * nq + i)),
                  pl.BlockSpec((seq, fw), lambda b, h, i: (b, h)),
                  pl.BlockSpec((fw, seq), lambda b, h, i: (h, b))],
        out_specs=pl.BlockSpec((tq, fw), lambda b, h, i: (b * nq + i, h)),
        out_shape=jax.ShapeDtypeStruct((n, width), BF16),
        scratch_shapes=[pltpu.VMEM((fw, tq), F32), pltpu.VMEM((2, tq, tq), F32),
                        pltpu.VMEM((tq, tq), BF16)]
        + [pltpu.VMEM((1, tq), F32)] * (2 * n_heads) + [pltpu.VMEM((tq, tq), F32)] * n_heads
        + [pltpu.VMEM((tq, tq), BF16)] * (2 * n_heads),
        compiler_params=_params(("parallel", "parallel", "arbitrary")),
        name="sb_attn",
    )(q, k, v)


def _diff_kernel(slopes_ref, lam_ref, g_ref, q_ref, k_ref, v_ref, o_ref, bias_ref,
                 s0_ref, s1_ref, mb0_ref, mb1_ref, m0_ref, m1_ref, acc0_ref, acc1_ref, kmax_ref,
                 *, tq, tk, lambda_init):
    h = pl.program_id(1)
    qi = pl.program_id(2)
    slope = slopes_ref[0, h]
    inv_slope = slopes_ref[1, h]
    nsub = tk // tq
    s_refs, mb_refs = (s0_ref, s1_ref), (mb0_ref, mb1_ref)
    m_refs, acc_refs = (m0_ref, m1_ref), (acc0_ref, acc1_ref)

    @pl.when(qi == 0)
    def _():
        key = lax.broadcasted_iota(jnp.int32, (tk, tq), 0)
        query = lax.broadcasted_iota(jnp.int32, (tk, tq), 1)
        key_ahead = key - query
        rel_bias = slope * key_ahead.astype(F32)
        bias_ref[nsub] = rel_bias
        for r in range(nsub):
            bias_ref[r] = jnp.where(key_ahead <= r * tq, rel_bias, MASKED)
        kf = k_ref[...].astype(F32)
        kmax_ref[0] = jnp.max(jnp.sum(kf * kf, axis=1, keepdims=True))

    feat = lax.broadcasted_iota(jnp.int32, (LANES, tq), 0)
    q = q_ref[...]
    q_maps = [jnp.where((feat >= c * HEAD_DIM) & (feat < (c + 1) * HEAD_DIM), q, jnp.zeros_like(q))
              for c in range(2)]

    for c in range(2):
        m_refs[c][...] = jnp.full_like(m_refs[c], MASKED)
        acc_refs[c][...] = jnp.zeros_like(acc_refs[c])

    last = (qi * tq) // tk

    qf = q.astype(F32)
    qk = jnp.max(jnp.sqrt(jnp.sum(qf * qf, axis=0, keepdims=True) * kmax_ref[0]))
    skip_dist = (EXP2_UNDERFLOW + 2.01 * qk) * inv_slope
    first = jnp.clip(((qi * tq).astype(F32) - skip_dist) * (1.0 / tk), 0.0, last.astype(F32))
    first = first.astype(jnp.int32)

    def offset(kb):
        return -slope * (qi * tq - kb * tk).astype(F32)

    half = tk // 2

    def scores(c, kb):
        sel = jnp.where(kb == last, qi % nsub, nsub)
        mb = None
        for part in range(2):
            rows = pl.ds(part * half, half)
            start = pl.multiple_of(kb * tk + part * half, half)
            s = _dot(k_ref[pl.ds(start, half), :], q_maps[c]) + bias_ref[sel, rows, :]
            s_refs[c][rows, :] = s
            part_max = jnp.max(s, axis=0, keepdims=True)
            mb = part_max if mb is None else jnp.maximum(mb, part_max)
        mb_refs[c][...] = mb + offset(kb)

    def accumulate(c, kb):
        m_old = m_refs[c][...]
        m_new = jnp.maximum(m_old, mb_refs[c][...])
        alpha = jnp.exp2(m_old - m_new)
        shift = m_new - offset(kb)
        pv = None
        for part in range(2):
            start = pl.multiple_of(kb * tk + part * half, half)
            p = jnp.exp2(s_refs[c][pl.ds(part * half, half), :] - shift)
            part_pv = _dot(v_ref[:, pl.ds(start, half)], p.astype(BF16))
            pv = part_pv if pv is None else pv + part_pv
        acc_refs[c][...] = alpha * acc_refs[c][...] + pv
        m_refs[c][...] = m_new

    scores(0, first)

    def step(kb):
        scores(1, kb)
        accumulate(0, kb)
        scores(0, jnp.minimum(kb + 1, last))
        accumulate(1, kb)

    def pair(j, carry):
        step(first + 2 * j)
        step(first + 2 * j + 1)
        return carry

    n_blocks = last + 1 - first
    lax.fori_loop(0, n_blocks // 2, pair, 0)

    @pl.when(n_blocks % 2 == 1)
    def _():
        step(last)

    lp = lam_ref[...]
    lam = (jnp.exp(jnp.sum(lp[0:1] * lp[1:2], axis=-1, keepdims=True))
           - jnp.exp(jnp.sum(lp[2:3] * lp[3:4], axis=-1, keepdims=True)) + lambda_init)
    a0, a1 = acc0_ref[...], acc1_ref[...]
    o = (a0[:LANES] * (1.0 / a0[LANES:LANES + 1])
         - lam * (a1[:LANES] * (1.0 / a1[LANES:LANES + 1])))
    ms = jnp.mean(o * o, axis=0, keepdims=True)
    o = ((o * lax.rsqrt(ms + NORM_EPS)) * g_ref[...]) * (1.0 - lambda_init)
    o_ref[...] = o.T.astype(o_ref.dtype)


def _diff_call(q, k, v, lam_params, subln_g, *, batch, tq, tk, lambda_init):
    n, width = k.shape
    heads = width // LANES
    assert tk % tq == 0
    slopes = [2.0 ** (-8.0 * (h + 1) / heads) * LOG2E for h in range(heads)]
    slopes = jnp.asarray([slopes, [1.0 / s for s in slopes]], dtype=F32)
    seq = n // batch
    nq = seq // tq
    return pl.pallas_call(
        functools.partial(_diff_kernel, tq=tq, tk=tk, lambda_init=lambda_init),
        grid=(batch, heads, nq),
        in_specs=[pl.BlockSpec(memory_space=pltpu.SMEM),
                  pl.BlockSpec(lam_params.shape, lambda b, h, i: (0, 0)),
                  pl.BlockSpec((LANES, 1), lambda b, h, i: (0, 0)),
                  pl.BlockSpec((LANES, tq), lambda b, h, i: (h, b * nq + i)),
                  pl.BlockSpec((seq, LANES), lambda b, h, i: (b, h)),
                  pl.BlockSpec((V_ROWS, seq), lambda b, h, i: (h, b))],
        out_specs=pl.BlockSpec((tq, LANES), lambda b, h, i: (b * nq + i, h)),
        out_shape=jax.ShapeDtypeStruct((n, width), BF16),
        scratch_shapes=[pltpu.VMEM((tk // tq + 1, tk, tq), F32)]
        + [pltpu.VMEM((tk, tq), F32)] * 2 + [pltpu.VMEM((1, tq), F32)] * 4
        + [pltpu.VMEM((V_ROWS, tq), F32)] * 2 + [pltpu.SMEM((1,), F32)],
        compiler_params=_params(("parallel", "parallel", "arbitrary")),
        name="diff_attn",
    )(slopes, lam_params.astype(F32), subln_g.reshape(LANES, 1).astype(F32), q, k, v)


def _merge_kernel(*refs, router):
    (x_ref, ysb_ref, ydf_ref, gmix_ref, wg_ref, bg_ref, wsb_ref, wdf_ref, wo_ref,
     gffn_ref) = refs[:10]
    rest = refs[10:]
    x = x_ref[...]
    d = x.shape[-1]
    h = _rms(x, gmix_ref[...]).astype(BF16)
    gates = jax.nn.sigmoid(_dot(h, wg_ref[...]) + bg_ref[...])
    merged = (gates[:, :d] * _dot(ysb_ref[...], wsb_ref[...])
              + gates[:, d:] * _dot(ydf_ref[...], wdf_ref[...]))
    xn = x + _dot(merged.astype(BF16), wo_ref[...])
    h2 = _rms(xn, gffn_ref[...])
    if router:
        wr_ref, xo_ref, h2_ref, lg_ref = rest
        lg_ref[...] = _dot(h2.astype(BF16), wr_ref[...])
        _store_token_tiles(h2_ref, h2)
    else:
        xo_ref, h2_ref = rest
        h2_ref[...] = h2.astype(h2_ref.dtype)
    xo_ref[...] = xn


def _merge_call(x, ysb, ydf, gmix, wg, bg, wsb, wdf, wo, gffn, w_router, *, tm):
    n, d = x.shape
    router = w_router is not None
    row = lambda w: pl.BlockSpec((tm, w), lambda i: (i, 0))
    const = lambda a: pl.BlockSpec(a.shape, lambda i: (0, 0))
    args = [x, ysb, ydf, gmix.reshape(1, d), wg, bg.reshape(1, -1), wsb, wdf, wo,
            gffn.reshape(1, d)]
    in_specs = [row(d), row(ysb.shape[1]), row(ydf.shape[1])] + [const(a) for a in args[3:]]
    out_specs = [row(d), row(d)]
    out_shape = [jax.ShapeDtypeStruct((n, d), F32), jax.ShapeDtypeStruct((n, d), BF16)]
    if router:
        assert d == TOKEN_TILE_ROWS * LANES
        out_specs[1] = pl.BlockSpec((tm * TOKEN_TILE_ROWS, LANES), lambda i: (i, 0))
        out_shape[1] = jax.ShapeDtypeStruct((n * TOKEN_TILE_ROWS, LANES), F32)
        wr = jnp.zeros((d, LANES), BF16).at[:, :w_router.shape[1]].set(w_router.astype(BF16))
        args.append(wr)
        in_specs.append(const(wr))
        out_specs.append(row(LANES))
        out_shape.append(jax.ShapeDtypeStruct((n, LANES), F32))
    return pl.pallas_call(
        functools.partial(_merge_kernel, router=router),
        grid=(n // tm,),
        in_specs=in_specs,
        out_specs=out_specs,
        out_shape=out_shape,
        compiler_params=_params(("parallel",)),
        name="merge_router" if router else "merge",
    )(*args)


def _swiglu_kernel(x_ref, h_ref, wg_ref, wu_ref, wd_ref, o_ref, acc_ref):
    f = pl.program_id(1)

    @pl.when(f == 0)
    def _():
        acc_ref[...] = jnp.zeros_like(acc_ref)

    h = h_ref[...]
    g = _dot(h, wg_ref[...])
    u = _dot(h, wu_ref[...])
    act = (g * jax.nn.sigmoid(g)) * u
    acc_ref[...] += _dot(act.astype(BF16), wd_ref[...])

    @pl.when(f == pl.num_programs(1) - 1)
    def _():
        o_ref[...] = x_ref[...] + acc_ref[...]


def _swiglu_call(x, h, w_gate_up, w_down, *, tm, nf):
    n, d = x.shape
    d_ff = w_down.shape[0]
    tf = d_ff // nf
    assert tf * nf == d_ff and tf % LANES == 0
    row = pl.BlockSpec((tm, d), lambda i, f: (i, 0))
    return pl.pallas_call(
        _swiglu_kernel,
        grid=(n // tm, nf),
        in_specs=[row, row,
                  pl.BlockSpec((d, tf), lambda i, f: (0, f)),
                  pl.BlockSpec((d, tf), lambda i, f: (0, f + nf)),
                  pl.BlockSpec((tf, d), lambda i, f: (f, 0))],
        out_specs=row,
        out_shape=jax.ShapeDtypeStruct((n, d), F32),
        scratch_shapes=[pltpu.VMEM((tm, d), F32)],
        compiler_params=_params(("parallel", "arbitrary")),
        name="swiglu",
    )(x, h, w_gate_up, w_gate_up, w_down)


def _expert_kernel(be_ref, tok_ref, dst_ref, meta_ref, h_hbm, sw_ref, w1_ref, w3_ref, w2_ref,
                   y_hbm, xbuf, acc_ref, ybuf, zrow_ref, gsem, ssem, *, n_spare):
    del be_ref
    i = pl.program_id(0)
    f = pl.program_id(1)
    nf = pl.num_programs(1)
    nused = meta_ref[0]
    rows = acc_ref.shape[0]

    def tile_of(row):
        return pl.ds(pl.multiple_of(row * TOKEN_TILE_ROWS, TOKEN_TILE_ROWS), TOKEN_TILE_ROWS)

    def gather_copy(blk, r):
        return pltpu.make_async_copy(h_hbm.at[tile_of(tok_ref[blk * rows + r])],
                                     xbuf.at[blk % 2, tile_of(r)], gsem.at[blk % 2])

    def scatter_copy(blk, r):
        return pltpu.make_async_copy(ybuf.at[(blk + 2) % 2, tile_of(r)],
                                     y_hbm.at[tile_of(dst_ref[(blk + 1) * rows + r])],
                                     ssem.at[(blk + 2) % 2])

    def for_rows(lo, n, fn):
        def body(r, carry):
            fn(lo + r)
            return carry
        lax.fori_loop(0, n, body, 0, unroll=8)

    @pl.when(jnp.logical_and(i == 0, f == 0))
    def _():
        ybuf[1] = jnp.zeros(ybuf.shape[1:], ybuf.dtype)
        for_rows(0, rows, lambda r: gather_copy(0, r).start())

    @pl.when(jnp.logical_and(f == 0, i <= nused))
    def _():
        for_rows(0, rows, lambda r: gather_copy(i, r).wait())

    @pl.when(jnp.logical_and(f == 0, i < nused))
    def _():
        acc_ref[...] = jnp.zeros_like(acc_ref)

    chunk = rows // nf

    @pl.when(i < nused)
    def _():
        for r in range(chunk):
            gather_copy(i + 1, f * chunk + r).start()
            scatter_copy(i - 1, f * chunk + r).start()
        xb = _load_token_tiles(xbuf.at[i % 2], rows).astype(BF16)
        g = _dot(xb, w1_ref[...])
        u = _dot(xb, w3_ref[...])
        hid = (g * jax.nn.sigmoid(g)) * u
        acc_ref[...] += _dot(hid.astype(BF16), w2_ref[...])

    @pl.when(i == nused)
    def _():
        for_rows(f * chunk, chunk, lambda r: scatter_copy(i - 1, r).start())

    @pl.when(f == nf - 1)
    def _():
        @pl.when(jnp.logical_and(i < nused, i >= 1))
        def _():
            for_rows(0, rows, lambda r: scatter_copy(i - 2, r).wait())

        @pl.when(i < nused)
        def _():
            _store_token_tiles(ybuf.at[i % 2], acc_ref[...] * sw_ref[...])

        @pl.when(i == nused)
        def _():
            @pl.when(i >= 1)
            def _():
                for_rows(0, rows, lambda r: scatter_copy(i - 2, r).wait())
            for_rows(0, rows, lambda r: scatter_copy(i - 1, r).wait())

            zrow_ref[...] = jnp.zeros_like(zrow_ref)
            n_pad = meta_ref[1]
            spare_lo = y_hbm.shape[0] // TOKEN_TILE_ROWS - rows - n_spare

            def zero_copy(r):
                return pltpu.make_async_copy(zrow_ref, y_hbm.at[tile_of(spare_lo + r)],
                                             ssem.at[0])

            def fill(r, carry):
                zero_copy(r).start()
                return carry

            def drain(r, carry):
                zero_copy(r).wait()
                return carry

            lax.fori_loop(n_pad, n_spare, fill, 0)
            lax.fori_loop(n_pad, n_spare, drain, 0)


def _expert_call(block_e, slot_tok, dst_all, meta, h, slot_w, w1, w3, w2, *, n_spare, nf):
    n_slots = slot_tok.shape[0]
    n_tok = h.shape[0] // TOKEN_TILE_ROWS
    n_out = n_tok * TOP_K + n_spare + MOE_BLOCK
    n_blocks = n_slots // MOE_BLOCK
    d = w1.shape[1]
    d_ff = w1.shape[2]
    tf = d_ff // nf
    tile_rows = MOE_BLOCK * TOKEN_TILE_ROWS
    assert tf * nf == d_ff and tf % LANES == 0 and MOE_BLOCK % nf == 0
    assert d == TOKEN_TILE_ROWS * LANES and h.shape[1] == LANES
    def tile(i, f, meta):
        return jnp.where(i < meta[0], f, nf - 1)

    grid_spec = pltpu.PrefetchScalarGridSpec(
        num_scalar_prefetch=4,
        grid=(n_blocks, nf),
        in_specs=[pl.BlockSpec(memory_space=pl.ANY),
                  pl.BlockSpec((MOE_BLOCK, 1), lambda i, f, be, *_: (i, 0)),
                  pl.BlockSpec((None, d, tf), lambda i, f, be, tok, dst, meta:
                               (be[i], 0, tile(i, f, meta))),
                  pl.BlockSpec((None, d, tf), lambda i, f, be, tok, dst, meta:
                               (be[i], 0, tile(i, f, meta))),
                  pl.BlockSpec((None, tf, d), lambda i, f, be, tok, dst, meta:
                               (be[i], tile(i, f, meta), 0))],
        out_specs=pl.BlockSpec(memory_space=pl.ANY),
        scratch_shapes=[pltpu.VMEM((2, tile_rows, LANES), F32),
                        pltpu.VMEM((MOE_BLOCK, d), F32),
                        pltpu.VMEM((2, tile_rows, LANES), F32),
                        pltpu.VMEM((TOKEN_TILE_ROWS, LANES), F32),
                        pltpu.SemaphoreType.DMA((2,)),
                        pltpu.SemaphoreType.DMA((2,))],
    )
    return pl.pallas_call(
        functools.partial(_expert_kernel, n_spare=n_spare),
        grid_spec=grid_spec,
        out_shape=jax.ShapeDtypeStruct((n_out * TOKEN_TILE_ROWS, LANES), F32),
        compiler_params=_params(("arbitrary", "arbitrary")),
        name="moe_experts",
    )(block_e, slot_tok, dst_all, meta, h, slot_w, w1, w3, w2)


def _combine_kernel(x_ref, y0_ref, y1_ref, o_ref):
    tm = x_ref.shape[0]
    o_ref[...] = x_ref[...] + (_load_token_tiles(y0_ref, tm) + _load_token_tiles(y1_ref, tm))


def _combine_call(x, y, *, tm):
    n, d = x.shape
    steps = n // tm
    y_rows = tm * TOKEN_TILE_ROWS
    return pl.pallas_call(
        _combine_kernel,
        grid=(steps,),
        in_specs=[pl.BlockSpec((tm, d), lambda i: (i, 0)),
                  pl.BlockSpec((y_rows, LANES), lambda i: (i, 0)),
                  pl.BlockSpec((y_rows, LANES), lambda i: (i + steps, 0))],
        out_specs=pl.BlockSpec((tm, d), lambda i: (i, 0)),
        out_shape=jax.ShapeDtypeStruct((n, d), F32),
        compiler_params=_params(("parallel",)),
        name="moe_combine",
    )(x, y, y)


def _route(logits):
    n_tok = logits.shape[0]
    n_assign = n_tok * TOP_K
    n_slots = -(-n_assign // MOE_BLOCK) * MOE_BLOCK + N_EXPERTS * MOE_BLOCK
    n_blocks = n_slots // MOE_BLOCK
    top_logit, top_idx = lax.top_k(logits, TOP_K)
    top_w = jax.nn.softmax(top_logit, axis=-1)
    onehot = jnp.sum(jax.nn.one_hot(top_idx, N_EXPERTS, dtype=jnp.int32), axis=1)
    csum = jnp.cumsum(onehot, axis=0)
    counts = csum[-1]
    padded = (counts + MOE_BLOCK - 1) // MOE_BLOCK * MOE_BLOCK
    padded_end = jnp.cumsum(padded)
    padded_start = padded_end - padded
    rank = jnp.take_along_axis(csum - onehot, top_idx, axis=1)
    slots = (padded_start[top_idx] + rank).astype(jnp.int32)
    dst = (jnp.arange(TOP_K, dtype=jnp.int32)[None, :] * n_tok
           + jnp.arange(n_tok, dtype=jnp.int32)[:, None])
    slot_dst = jnp.full((n_slots,), -1, jnp.int32).at[slots.reshape(-1)].set(dst.reshape(-1))
    real = slot_dst >= 0
    n_spare = n_slots - n_assign
    pad_rank = jnp.cumsum(jnp.logical_not(real).astype(jnp.int32)) - 1
    slot_tok = jnp.where(real, slot_dst % n_tok, 0)
    w_kmajor = top_w.T.reshape(-1)
    slot_w = jnp.where(real, w_kmajor[jnp.maximum(slot_dst, 0)], 0.0)
    slot_dst = jnp.where(real, slot_dst, n_assign + pad_rank)
    virtual = n_assign + n_spare + jnp.arange(MOE_BLOCK, dtype=jnp.int32)
    dst_all = jnp.concatenate([virtual, slot_dst])
    block_e = jnp.minimum(
        jnp.searchsorted(padded_end, jnp.arange(n_blocks) * MOE_BLOCK, side='right'),
        N_EXPERTS - 1).astype(jnp.int32)
    meta = jnp.stack([padded_end[-1] // MOE_BLOCK, padded_end[-1] - n_assign]).astype(jnp.int32)
    return block_e, slot_tok, dst_all, meta, slot_w.reshape(n_slots, 1), n_spare


class _Tiles(NamedTuple):
    rows: int
    sb_q: int
    sb_heads: int
    diff_q: int
    diff_k: int
    ff_splits: int


def _tiles(n, seq, branch_width):
    return _Tiles(rows=min(512, n), sb_q=min(256, seq), sb_heads=branch_width // HEAD_DIM,
                  diff_q=min(1024, seq), diff_k=min(1024, seq), ff_splits=2)


def kernel(x, norm_mix_g, w_in, diff_q_norm_g, diff_k_norm_g, diff_lambda, diff_subln_g,
           w_gate, b_gate, w_branch_sb, w_branch_diff, w_out, norm_ffn_g,
           ffn_w_gate_up, ffn_w_down, moe_w_router, moe_w1, moe_w3, moe_w2):
    batch, seq, d = x.shape
    n = batch * seq
    depth = w_in.shape[0]
    t = _tiles(n, seq, w_in.shape[2] // 6)
    xf = x.reshape(n, d).astype(F32)
    bf = lambda w: w.astype(BF16)
    for i in range(depth):
        sbk, dfk, sbq, sbv, dfq, dfv = _proj_call(
            xf, norm_mix_g[i], w_in[i], diff_q_norm_g[i], diff_k_norm_g[i], tm=t.rows)
        y_sb = _sb_call(sbq, sbk, sbv, batch=batch, tq=t.sb_q, n_heads=t.sb_heads)
        lambda_init = 0.8 - 0.6 * math.exp(-0.3 * i)
        y_df = _diff_call(dfq, dfk, dfv, diff_lambda[i], diff_subln_g[i],
                          batch=batch, tq=t.diff_q, tk=t.diff_k, lambda_init=lambda_init)
        j = i // 2
        dense = i % 2 == 0
        outs = _merge_call(xf, y_sb, y_df, norm_mix_g[i], bf(w_gate[i]), b_gate[i],
                           bf(w_branch_sb[i]), bf(w_branch_diff[i]), bf(w_out[i]),
                           norm_ffn_g[i], None if dense else moe_w_router[j], tm=t.rows)
        if dense:
            xf, h2 = outs
            xf = _swiglu_call(xf, h2, bf(ffn_w_gate_up[j]), bf(ffn_w_down[j]),
                              tm=t.rows, nf=t.ff_splits)
        else:
            xf, h2, logits = outs
            block_e, slot_tok, dst_all, meta, slot_w, n_spare = _route(logits[:, :N_EXPERTS])
            y = _expert_call(block_e, slot_tok, dst_all, meta, h2, slot_w,
                             bf(moe_w1[j]), bf(moe_w3[j]), bf(moe_w2[j]),
                             n_spare=n_spare, nf=t.ff_splits)
            xf = _combine_call(xf, y, tm=t.rows)
    return xf.reshape(batch, seq, d).astype(x.dtype)
```

```python
import functools
import math
from typing import NamedTuple

import jax
import jax.numpy as jnp
from jax import lax
from jax.experimental import pallas as pl
from jax.experimental.pallas import tpu as pltpu

F32 = jnp.float32
BF16 = jnp.bfloat16

HEAD_DIM = 64
NORM_EPS = 1e-6
N_EXPERTS = 8
TOP_K = 2
MOE_BLOCK = 512
LANES = 128
TOKEN_TILE_ROWS = 8
VMEM_LIMIT = 56 * 1024 * 1024

EXP_UNDERFLOW = -104.0
EXP2_UNDERFLOW = 152.0
MASKED = -1e30
LOG2E = math.log2(math.e)
V_ROWS = LANES + 16


def _dot(a, b):
    return jnp.dot(a, b, preferred_element_type=F32)


def _dot_nt(a, b):
    return lax.dot_general(a, b, (((1,), (1,)), ((), ())), preferred_element_type=F32)


def _rms(x, g):
    ms = jnp.mean(x * x, axis=-1, keepdims=True)
    return (x * lax.rsqrt(ms + NORM_EPS)) * g


def _split(x):
    hi = x.astype(BF16)
    return hi, (x - hi.astype(F32)).astype(BF16)


def _store_token_tiles(ref, x):
    rows = x.shape[0]
    for j in range(TOKEN_TILE_ROWS):
        ref[pl.ds(j, rows, stride=TOKEN_TILE_ROWS), :] = x[:, j * LANES:(j + 1) * LANES]


def _load_token_tiles(ref, rows):
    return jnp.concatenate([ref[pl.ds(j, rows, stride=TOKEN_TILE_ROWS), :]
                            for j in range(TOKEN_TILE_ROWS)], axis=1)


def _params(sem, vmem=VMEM_LIMIT):
    return pltpu.CompilerParams(dimension_semantics=sem, vmem_limit_bytes=vmem)


def _proj_kernel(x_ref, g_ref, wk_ref, wt_ref, bd_ref, gq_ref, gk_ref,
                 sbk_ref, dfk_ref, sbq_ref, sbv_ref, dfq_ref, dfv_ref, *, width):
    scale = HEAD_DIM ** -0.5
    h = _rms(x_ref[...], g_ref[...]).astype(BF16)
    bd = bd_ref[...]

    def feature_major(j):
        return _dot_nt(wt_ref[j * width:(j + 1) * width, :], h)

    sbk_ref[...] = _dot(h, wk_ref[:, :width]).astype(BF16)
    k = _dot(h, wk_ref[:, width:])
    hi, lo = _split(k * k)
    ss = _dot(hi, bd) + _dot(lo, bd)
    dfk_ref[...] = ((k * lax.rsqrt(ss * (1.0 / HEAD_DIM) + NORM_EPS)) * gk_ref[...]).astype(BF16)

    sbq_ref[...] = (feature_major(0) * scale).astype(BF16)
    sbv_ref[...] = feature_major(1).astype(BF16)
    q = feature_major(2)
    hi, lo = _split(q * q)
    ss = _dot(bd, hi) + _dot(bd, lo)
    q = (q * lax.rsqrt(ss * (1.0 / HEAD_DIM) + NORM_EPS)) * gq_ref[...]
    dfq_ref[...] = (q * (scale * LOG2E)).astype(BF16)
    v = feature_major(3).astype(BF16)
    for hd in range(width // LANES):
        dfv_ref[hd * V_ROWS:hd * V_ROWS + LANES, :] = v[hd * LANES:(hd + 1) * LANES, :]
        dfv_ref[hd * V_ROWS + LANES:(hd + 1) * V_ROWS, :] = jnp.ones(
            (V_ROWS - LANES, v.shape[1]), BF16)


def _proj_call(x, g, w_in, gq, gk, *, tm):
    n, d = x.shape
    width = w_in.shape[1] // 6
    cols = lambda j: w_in[:, j * width:(j + 1) * width]
    wk = jnp.concatenate([cols(1), cols(4)], axis=1).astype(BF16)
    wt = jnp.concatenate([cols(0), cols(2), cols(3), cols(5)], axis=1).T.astype(BF16)
    grp = lax.broadcasted_iota(jnp.int32, (width, width), 0) // HEAD_DIM
    bd = (grp == grp.T).astype(BF16)
    const = lambda a: pl.BlockSpec(a.shape, lambda i: (0, 0))
    tok_major = pl.BlockSpec((tm, width), lambda i: (i, 0))
    feat_major = pl.BlockSpec((width, tm), lambda i: (0, i))
    v_rows = width // LANES * V_ROWS
    args = [x, g.reshape(1, d), wk, wt, bd,
            jnp.tile(gq, width // HEAD_DIM).reshape(width, 1),
            jnp.tile(gk, width // HEAD_DIM).reshape(1, width)]
    return pl.pallas_call(
        functools.partial(_proj_kernel, width=width),
        grid=(n // tm,),
        in_specs=[pl.BlockSpec((tm, d), lambda i: (i, 0))] + [const(a) for a in args[1:]],
        out_specs=[tok_major] * 2 + [feat_major] * 3 + [pl.BlockSpec((v_rows, tm), lambda i: (0, i))],
        out_shape=[jax.ShapeDtypeStruct((n, width), BF16)] * 2
        + [jax.ShapeDtypeStruct((width, n), BF16)] * 3 + [jax.ShapeDtypeStruct((v_rows, n), BF16)],
        compiler_params=_params(("parallel",)),
        name="proj",
    )(*args)


def _sb_kernel(q_ref, k_ref, v_ref, o_ref, acc_ref, mask_ref, later_ref, *scratch, tq, n_heads):
    qi = pl.program_id(2)
    l_refs, row_refs, lb_refs, hi_refs, lo_refs = (
        scratch[j * n_heads:(j + 1) * n_heads] for j in range(5))

    @pl.when(qi == 0)
    def _():
        key = lax.broadcasted_iota(jnp.int32, (tq, tq), 0)
        other = lax.broadcasted_iota(jnp.int32, (tq, tq), 1)
        mask_ref[0] = jnp.where(key < other, 1.0, 0.0).astype(F32)
        mask_ref[1] = jnp.ones((tq, tq), F32)
        later_ref[...] = jnp.where(other > key, 1.0, 0.0).astype(BF16)

    feat = lax.broadcasted_iota(jnp.int32, q_ref.shape, 0)
    q = q_ref[...]
    q_heads = [jnp.where((feat >= hh * HEAD_DIM) & (feat < (hh + 1) * HEAD_DIM), q,
                         jnp.zeros_like(q)) for hh in range(n_heads)]

    acc_ref[...] = jnp.zeros_like(acc_ref)
    for hh in range(n_heads):
        l_refs[hh][...] = jnp.zeros_like(l_refs[hh])

    def logits(hh, kb):
        start = pl.multiple_of(kb * tq, tq)
        z = _dot(k_ref[pl.ds(start, tq), :], q_heads[hh])
        sp = jnp.maximum(z, 0.0) + jnp.log(1.0 + jnp.exp(-jnp.abs(z)))
        log_keep = -sp * mask_ref[jnp.where(kb == qi, 0, 1)]
        hi, lo = _split(log_keep)
        hi_refs[hh][...] = hi
        lo_refs[hh][...] = lo
        lb_refs[hh][...] = z - sp
        row_refs[hh][...] = log_keep[:1, :]

    def weights(hh, kb):
        feats = slice(hh * HEAD_DIM, (hh + 1) * HEAD_DIM)
        start = pl.multiple_of(kb * tq, tq)
        later_keys = later_ref[...]
        later = _dot(later_keys, hi_refs[hh][...]) + _dot(later_keys, lo_refs[hh][...])
        carry_l = l_refs[hh][...]
        w = jnp.exp(lb_refs[hh][...] + later + carry_l) * mask_ref[jnp.where(kb == qi, 0, 1)]
        acc_ref[feats, :] += _dot(v_ref[feats, pl.ds(start, tq)], w.astype(BF16))
        l_new = carry_l + later[:1, :] + row_refs[hh][...]
        l_refs[hh][...] = l_new
        return jnp.max(l_new)

    logits(0, qi)

    def cond(c):
        it, lm = c
        return jnp.logical_and(it <= qi, lm > EXP_UNDERFLOW)

    def body(c):
        it, _ = c
        kb = qi - it
        lm = None
        for hh in range(n_heads):
            if hh + 1 < n_heads:
                logits(hh + 1, kb)
            else:
                logits(0, jnp.maximum(kb - 1, 0))
            head_max = weights(hh, kb)
            lm = head_max if lm is None else jnp.maximum(lm, head_max)
        return it + 1, lm

    lax.while_loop(cond, body, (jnp.int32(0), jnp.float32(0.0)))
    o_ref[...] = acc_ref[...].T.astype(o_ref.dtype)


def _sb_call(q, k, v, *, batch, tq, n_heads):
    n, width = k.shape
    seq = n // batch
    nq = seq // tq
    fw = n_heads * HEAD_DIM
    return pl.pallas_call(
        functools.partial(_sb_kernel, tq=tq, n_heads=n_heads),
        grid=(batch, width // fw, nq),
        in_specs=[pl.BlockSpec((fw, tq), lambda b, h, i: (h, b * nq + i)),
                  pl.BlockSpec((seq, fw), lambda b, h, i: (b, h)),
                  pl.BlockSpec((fw, seq), lambda b, h, i: (h, b))],
        out_specs=pl.BlockSpec((tq, fw), lambda b, h, i: (b * nq + i, h)),
        out_shape=jax.ShapeDtypeStruct((n, width), BF16),
        scratch_shapes=[pltpu.VMEM((fw, tq), F32), pltpu.VMEM((2, tq, tq), F32),
                        pltpu.VMEM((tq, tq), BF16)]
        + [pltpu.VMEM((1, tq), F32)] * (2 * n_heads) + [pltpu.VMEM((tq, tq), F32)] * n_heads
        + [pltpu.VMEM((tq, tq), BF16)] * (2 * n_heads),
        compiler_params=_params(("parallel", "parallel", "arbitrary")),
        name="sb_attn",
    )(q, k, v)


def _diff_kernel(slopes_ref, lam_ref, g_ref, q_ref, k_ref, v_ref, o_ref, bias_ref,
                 s0_ref, s1_ref, mb0_ref, mb1_ref, m0_ref, m1_ref, acc0_ref, acc1_ref, kmax_ref,
                 *, tq, tk, lambda_init):
    h = pl.program_id(1)
    qi = pl.program_id(2)
    slope = slopes_ref[0, h]
    inv_slope = slopes_ref[1, h]
    nsub = tk // tq
    s_refs, mb_refs = (s0_ref, s1_ref), (mb0_ref, mb1_ref)
    m_refs, acc_refs = (m0_ref, m1_ref), (acc0_ref, acc1_ref)

    @pl.when(qi == 0)
    def _():
        key = lax.broadcasted_iota(jnp.int32, (tk, tq), 0)
        query = lax.broadcasted_iota(jnp.int32, (tk, tq), 1)
        key_ahead = key - query
        rel_bias = slope * key_ahead.astype(F32)
        bias_ref[nsub] = rel_bias
        for r in range(nsub):
            bias_ref[r] = jnp.where(key_ahead <= r * tq, rel_bias, MASKED)
        kf = k_ref[...].astype(F32)
        kmax_ref[0] = jnp.max(jnp.sum(kf * kf, axis=1, keepdims=True))

    feat = lax.broadcasted_iota(jnp.int32, (LANES, tq), 0)
    q = q_ref[...]
    q_maps = [jnp.where((feat >= c * HEAD_DIM) & (feat < (c + 1) * HEAD_DIM), q, jnp.zeros_like(q))
              for c in range(2)]

    for c in range(2):
        m_refs[c][...] = jnp.full_like(m_refs[c], MASKED)
        acc_refs[c][...] = jnp.zeros_like(acc_refs[c])

    last = (qi * tq) // tk

    qf = q.astype(F32)
    qk = jnp.max(jnp.sqrt(jnp.sum(qf * qf, axis=0, keepdims=True) * kmax_ref[0]))
    skip_dist = (EXP2_UNDERFLOW + 2.01 * qk) * inv_slope
    first = jnp.clip(((qi * tq).astype(F32) - skip_dist) * (1.0 / tk), 0.0, last.astype(F32))
    first = first.astype(jnp.int32)

    def offset(kb):
        return -slope * (qi * tq - kb * tk).astype(F32)

    half = tk // 2

    def scores(c, kb):
        sel = jnp.where(kb == last, qi % nsub, nsub)
        mb = None
        for part in range(2):
            rows = pl.ds(part * half, half)
            start = pl.multiple_of(kb * tk + part * half, half)
            s = _dot(k_ref[pl.ds(start, half), :], q_maps[c]) + bias_ref[sel, rows, :]
            s_refs[c][rows, :] = s
            part_max = jnp.max(s, axis=0, keepdims=True)
            mb = part_max if mb is None else jnp.maximum(mb, part_max)
        mb_refs[c][...] = mb + offset(kb)

    def accumulate(c, kb):
        m_old = m_refs[c][...]
        m_new = jnp.maximum(m_old, mb_refs[c][...])
        alpha = jnp.exp2(m_old - m_new)
        shift = m_new - offset(kb)
        pv = None
        for part in range(2):
            start = pl.multiple_of(kb * tk + part * half, half)
            p = jnp.exp2(s_refs[c][pl.ds(part * half, half), :] - shift)
            part_pv = _dot(v_ref[:, pl.ds(start, half)], p.astype(BF16))
            pv = part_pv if pv is None else pv + part_pv
        acc_refs[c][...] = alpha * acc_refs[c][...] + pv
        m_refs[c][...] = m_new

    scores(0, first)

    def step(kb):
        scores(1, kb)
        accumulate(0, kb)
        scores(0, jnp.minimum(kb + 1, last))
        accumulate(1, kb)

    def pair(j, carry):
        step(first + 2 * j)
        step(first + 2 * j + 1)
        return carry

    n_blocks = last + 1 - first
    lax.fori_loop(0, n_blocks // 2, pair, 0)

    @pl.when(n_blocks % 2 == 1)
    def _():
        step(last)

    lp = lam_ref[...]
    lam = (jnp.exp(jnp.sum(lp[0:1] * lp[1:2], axis=-1, keepdims=True))
           - jnp.exp(jnp.sum(lp[2:3] * lp[3:4], axis=-1, keepdims=True)) + lambda_init)
    a0, a1 = acc0_ref[...], acc1_ref[...]
    o = (a0[:LANES] * (1.0 / a0[LANES:LANES + 1])
         - lam * (a1[:LANES] * (1.0 / a1[LANES:LANES + 1])))
    ms = jnp.mean(o * o, axis=0, keepdims=True)
    o = ((o * lax.rsqrt(ms + NORM_EPS)) * g_ref[...]) * (1.0 - lambda_init)
    o_ref[...] = o.T.astype(o_ref.dtype)


def _diff_call(q, k, v, lam_params, subln_g, *, batch, tq, tk, lambda_init):
    n, width = k.shape
    heads = width // LANES
    assert tk % tq == 0
    slopes = [2.0 ** (-8.0 * (h + 1) / heads) * LOG2E for h in range(heads)]
    slopes = jnp.asarray([slopes, [1.0 / s for s in slopes]], dtype=F32)
    seq = n // batch
    nq = seq // tq
    return pl.pallas_call(
        functools.partial(_diff_kernel, tq=tq, tk=tk, lambda_init=lambda_init),
        grid=(batch, heads, nq),
        in_specs=[pl.BlockSpec(memory_space=pltpu.SMEM),
                  pl.BlockSpec(lam_params.shape, lambda b, h, i: (0, 0)),
                  pl.BlockSpec((LANES, 1), lambda b, h, i: (0, 0)),
                  pl.BlockSpec((LANES, tq), lambda b, h, i: (h, b * nq + i)),
                  pl.BlockSpec((seq, LANES), lambda b, h, i: (b, h)),
                  pl.BlockSpec((V_ROWS, seq), lambda b, h, i: (h, b))],
        out_specs=pl.BlockSpec((tq, LANES), lambda b, h, i: (b * nq + i, h)),
        out_shape=jax.ShapeDtypeStruct((n, width), BF16),
        scratch_shapes=[pltpu.VMEM((tk // tq + 1, tk, tq), F32)]
        + [pltpu.VMEM((tk, tq), F32)] * 2 + [pltpu.VMEM((1, tq), F32)] * 4
        + [pltpu.VMEM((V_ROWS, tq), F32)] * 2 + [pltpu.SMEM((1,), F32)],
        compiler_params=_params(("parallel", "parallel", "arbitrary")),
        name="diff_attn",
    )(slopes, lam_params.astype(F32), subln_g.reshape(LANES, 1).astype(F32), q, k, v)


def _merge_kernel(*refs, router):
    (x_ref, ysb_ref, ydf_ref, gmix_ref, wg_ref, bg_ref, wsb_ref, wdf_ref, wo_ref,
     gffn_ref) = refs[:10]
    rest = refs[10:]
    x = x_ref[...]
    d = x.shape[-1]
    h = _rms(x, gmix_ref[...]).astype(BF16)
    gates = jax.nn.sigmoid(_dot(h, wg_ref[...]) + bg_ref[...])
    merged = (gates[:, :d] * _dot(ysb_ref[...], wsb_ref[...])
              + gates[:, d:] * _dot(ydf_ref[...], wdf_ref[...]))
    xn = x + _dot(merged.astype(BF16), wo_ref[...])
    h2 = _rms(xn, gffn_ref[...])
    if router:
        wr_ref, xo_ref, h2_ref, lg_ref = rest
        lg_ref[...] = _dot(h2.astype(BF16), wr_ref[...])
        _store_token_tiles(h2_ref, h2)
    else:
        xo_ref, h2_ref = rest
        h2_ref[...] = h2.astype(h2_ref.dtype)
    xo_ref[...] = xn


def _merge_call(x, ysb, ydf, gmix, wg, bg, wsb, wdf, wo, gffn, w_router, *, tm):
    n, d = x.shape
    router = w_router is not None
    row = lambda w: pl.BlockSpec((tm, w), lambda i: (i, 0))
    const = lambda a: pl.BlockSpec(a.shape, lambda i: (0, 0))
    args = [x, ysb, ydf, gmix.reshape(1, d), wg, bg.reshape(1, -1), wsb, wdf, wo,
            gffn.reshape(1, d)]
    in_specs = [row(d), row(ysb.shape[1]), row(ydf.shape[1])] + [const(a) for a in args[3:]]
    out_specs = [row(d), row(d)]
    out_shape = [jax.ShapeDtypeStruct((n, d), F32), jax.ShapeDtypeStruct((n, d), BF16)]
    if router:
        assert d == TOKEN_TILE_ROWS * LANES
        out_specs[1] = pl.BlockSpec((tm * TOKEN_TILE_ROWS, LANES), lambda i: (i, 0))
        out_shape[1] = jax.ShapeDtypeStruct((n * TOKEN_TILE_ROWS, LANES), F32)
        wr = jnp.zeros((d, LANES), BF16).at[:, :w_router.shape[1]].set(w_router.astype(BF16))
        args.append(wr)
        in_specs.append(const(wr))
        out_specs.append(row(LANES))
        out_shape.append(jax.ShapeDtypeStruct((n, LANES), F32))
    return pl.pallas_call(
        functools.partial(_merge_kernel, router=router),
        grid=(n // tm,),
        in_specs=in_specs,
        out_specs=out_specs,
        out_shape=out_shape,
        compiler_params=_params(("parallel",)),
        name="merge_router" if router else "merge",
    )(*args)


def _swiglu_kernel(x_ref, h_ref, wg_ref, wu_ref, wd_ref, o_ref, acc_ref):
    f = pl.program_id(1)

    @pl.when(f == 0)
    def _():
        acc_ref[...] = jnp.zeros_like(acc_ref)

    h = h_ref[...]
    g = _dot(h, wg_ref[...])
    u = _dot(h, wu_ref[...])
    act = (g * jax.nn.sigmoid(g)) * u
    acc_ref[...] += _dot(act.astype(BF16), wd_ref[...])

    @pl.when(f == pl.num_programs(1) - 1)
    def _():
        o_ref[...] = x_ref[...] + acc_ref[...]


def _swiglu_call(x, h, w_gate_up, w_down, *, tm, nf):
    n, d = x.shape
    d_ff = w_down.shape[0]
    tf = d_ff // nf
    assert tf * nf == d_ff and tf % LANES == 0
    row = pl.BlockSpec((tm, d), lambda i, f: (i, 0))
    mode = dict(pipeline_mode=pl.Buffered(1)) if nf == 1 else {}
    return pl.pallas_call(
        _swiglu_kernel,
        grid=(n // tm, nf),
        in_specs=[row, row,
                  pl.BlockSpec((d, tf), lambda i, f: (0, f), **mode),
                  pl.BlockSpec((d, tf), lambda i, f: (0, f + nf), **mode),
                  pl.BlockSpec((tf, d), lambda i, f: (f, 0), **mode)],
        out_specs=row,
        out_shape=jax.ShapeDtypeStruct((n, d), F32),
        scratch_shapes=[pltpu.VMEM((tm, d), F32)],
        compiler_params=_params(("parallel", "arbitrary")),
        name="swiglu",
    )(x, h, w_gate_up, w_gate_up, w_down)


def _expert_kernel(be_ref, tok_ref, dst_ref, meta_ref, h_hbm, sw_ref, w1_ref, w3_ref, w2_ref,
                   y_hbm, xbuf, acc_ref, ybuf, zrow_ref, gsem, ssem, *, n_spare):
    del be_ref
    i = pl.program_id(0)
    f = pl.program_id(1)
    nf = pl.num_programs(1)
    nused = meta_ref[0]
    rows = acc_ref.shape[0]

    def tile_of(row):
        return pl.ds(pl.multiple_of(row * TOKEN_TILE_ROWS, TOKEN_TILE_ROWS), TOKEN_TILE_ROWS)

    def gather_copy(blk, r):
        return pltpu.make_async_copy(h_hbm.at[tile_of(tok_ref[blk * rows + r])],
                                     xbuf.at[blk % 2, tile_of(r)], gsem.at[blk % 2])

    def scatter_copy(blk, r):
        return pltpu.make_async_copy(ybuf.at[(blk + 2) % 2, tile_of(r)],
                                     y_hbm.at[tile_of(dst_ref[(blk + 1) * rows + r])],
                                     ssem.at[(blk + 2) % 2])

    def for_rows(lo, n, fn):
        def body(r, carry):
            fn(lo + r)
            return carry
        lax.fori_loop(0, n, body, 0, unroll=8)

    @pl.when(jnp.logical_and(i == 0, f == 0))
    def _():
        ybuf[1] = jnp.zeros(ybuf.shape[1:], ybuf.dtype)
        for_rows(0, rows, lambda r: gather_copy(0, r).start())

    @pl.when(jnp.logical_and(f == 0, i <= nused))
    def _():
        for_rows(0, rows, lambda r: gather_copy(i, r).wait())

    @pl.when(jnp.logical_and(f == 0, i < nused))
    def _():
        acc_ref[...] = jnp.zeros_like(acc_ref)

    chunk = rows // nf

    @pl.when(i < nused)
    def _():
        for r in range(chunk):
            gather_copy(i + 1, f * chunk + r).start()
            scatter_copy(i - 1, f * chunk + r).start()
        xb = _load_token_tiles(xbuf.at[i % 2], rows).astype(BF16)
        g = _dot(xb, w1_ref[...])
        u = _dot(xb, w3_ref[...])
        hid = (g * jax.nn.sigmoid(g)) * u
        acc_ref[...] += _dot(hid.astype(BF16), w2_ref[...])

    @pl.when(i == nused)
    def _():
        for_rows(f * chunk, chunk, lambda r: scatter_copy(i - 1, r).start())

    @pl.when(f == nf - 1)
    def _():
        @pl.when(jnp.logical_and(i < nused, i >= 1))
        def _():
            for_rows(0, rows, lambda r: scatter_copy(i - 2, r).wait())

        @pl.when(i < nused)
        def _():
            _store_token_tiles(ybuf.at[i % 2], acc_ref[...] * sw_ref[...])

        @pl.when(i == nused)
        def _():
            @pl.when(i >= 1)
            def _():
                for_rows(0, rows, lambda r: scatter_copy(i - 2, r).wait())
            for_rows(0, rows, lambda r: scatter_copy(i - 1, r).wait())

            zrow_ref[...] = jnp.zeros_like(zrow_ref)
            n_pad = meta_ref[1]
            spare_lo = y_hbm.shape[0] // TOKEN_TILE_ROWS - rows - n_spare

            def zero_copy(r):
                return pltpu.make_async_copy(zrow_ref, y_hbm.at[tile_of(spare_lo + r)],
                                             ssem.at[0])

            def fill(r, carry):
                zero_copy(r).start()
                return carry

            def drain(r, carry):
                zero_copy(r).wait()
                return carry

            lax.fori_loop(n_pad, n_spare, fill, 0)
            lax.fori_loop(n_pad, n_spare, drain, 0)


def _expert_call(block_e, slot_tok, dst_all, meta, h, slot_w, w1, w3, w2, *, n_spare, nf):
    n_slots = slot_tok.shape[0]
    n_tok = h.shape[0] // TOKEN_TILE_ROWS
    n_out = n_tok * TOP_K + n_spare + MOE_BLOCK
    n_blocks = n_slots // MOE_BLOCK
    d = w1.shape[1]
    d_ff = w1.shape[2]
    tf = d_ff // nf
    tile_rows = MOE_BLOCK * TOKEN_TILE_ROWS
    assert tf * nf == d_ff and tf % LANES == 0 and MOE_BLOCK % nf == 0
    assert d == TOKEN_TILE_ROWS * LANES and h.shape[1] == LANES
    def tile(i, f, meta):
        return jnp.where(i < meta[0], f, nf - 1)

    grid_spec = pltpu.PrefetchScalarGridSpec(
        num_scalar_prefetch=4,
        grid=(n_blocks, nf),
        in_specs=[pl.BlockSpec(memory_space=pl.ANY),
                  pl.BlockSpec((MOE_BLOCK, 1), lambda i, f, be, *_: (i, 0)),
                  pl.BlockSpec((None, d, tf), lambda i, f, be, tok, dst, meta:
                               (be[i], 0, tile(i, f, meta))),
                  pl.BlockSpec((None, d, tf), lambda i, f, be, tok, dst, meta:
                               (be[i], 0, tile(i, f, meta))),
                  pl.BlockSpec((None, tf, d), lambda i, f, be, tok, dst, meta:
                               (be[i], tile(i, f, meta), 0))],
        out_specs=pl.BlockSpec(memory_space=pl.ANY),
        scratch_shapes=[pltpu.VMEM((2, tile_rows, LANES), F32),
                        pltpu.VMEM((MOE_BLOCK, d), F32),
                        pltpu.VMEM((2, tile_rows, LANES), F32),
                        pltpu.VMEM((TOKEN_TILE_ROWS, LANES), F32),
                        pltpu.SemaphoreType.DMA((2,)),
                        pltpu.SemaphoreType.DMA((2,))],
    )
    return pl.pallas_call(
        functools.partial(_expert_kernel, n_spare=n_spare),
        grid_spec=grid_spec,
        out_shape=jax.ShapeDtypeStruct((n_out * TOKEN_TILE_ROWS, LANES), F32),
        compiler_params=_params(("arbitrary", "arbitrary")),
        name="moe_experts",
    )(block_e, slot_tok, dst_all, meta, h, slot_w, w1, w3, w2)


def _combine_kernel(x_ref, y0_ref, y1_ref, o_ref):
    tm = x_ref.shape[0]
    o_ref[...] = x_ref[...] + (_load_token_tiles(y0_ref, tm) + _load_token_tiles(y1_ref, tm))


def _combine_call(x, y, *, tm):
    n, d = x.shape
    steps = n // tm
    y_rows = tm * TOKEN_TILE_ROWS
    return pl.pallas_call(
        _combine_kernel,
        grid=(steps,),
        in_specs=[pl.BlockSpec((tm, d), lambda i: (i, 0)),
                  pl.BlockSpec((y_rows, LANES), lambda i: (i, 0)),
                  pl.BlockSpec((y_rows, LANES), lambda i: (i + steps, 0))],
        out_specs=pl.BlockSpec((tm, d), lambda i: (i, 0)),
        out_shape=jax.ShapeDtypeStruct((n, d), F32),
        compiler_params=_params(("parallel",)),
        name="moe_combine",
    )(x, y, y)


def _route(logits):
    n_tok = logits.shape[0]
    n_assign = n_tok * TOP_K
    n_slots = -(-n_assign // MOE_BLOCK) * MOE_BLOCK + N_EXPERTS * MOE_BLOCK
    n_blocks = n_slots // MOE_BLOCK
    top_logit, top_idx = lax.top_k(logits, TOP_K)
    top_w = jax.nn.softmax(top_logit, axis=-1)
    onehot = jnp.sum(jax.nn.one_hot(top_idx, N_EXPERTS, dtype=jnp.int32), axis=1)
    csum = jnp.cumsum(onehot, axis=0)
    counts = csum[-1]
    padded = (counts + MOE_BLOCK - 1) // MOE_BLOCK * MOE_BLOCK
    padded_end = jnp.cumsum(padded)
    padded_start = padded_end - padded
    rank = jnp.take_along_axis(csum - onehot, top_idx, axis=1)
    slots = (padded_start[top_idx] + rank).astype(jnp.int32)
    dst = (jnp.arange(TOP_K, dtype=jnp.int32)[None, :] * n_tok
           + jnp.arange(n_tok, dtype=jnp.int32)[:, None])
    slot_dst = jnp.full((n_slots,), -1, jnp.int32).at[slots.reshape(-1)].set(dst.reshape(-1))
    real = slot_dst >= 0
    n_spare = n_slots - n_assign
    pad_rank = jnp.cumsum(jnp.logical_not(real).astype(jnp.int32)) - 1
    slot_tok = jnp.where(real, slot_dst % n_tok, 0)
    w_kmajor = top_w.T.reshape(-1)
    slot_w = jnp.where(real, w_kmajor[jnp.maximum(slot_dst, 0)], 0.0)
    slot_dst = jnp.where(real, slot_dst, n_assign + pad_rank)
    virtual = n_assign + n_spare + jnp.arange(MOE_BLOCK, dtype=jnp.int32)
    dst_all = jnp.concatenate([virtual, slot_dst])
    block_e = jnp.minimum(
        jnp.searchsorted(padded_end, jnp.arange(n_blocks) * MOE_BLOCK, side='right'),
        N_EXPERTS - 1).astype(jnp.int32)
    meta = jnp.stack([padded_end[-1] // MOE_BLOCK, padded_end[-1] - n_assign]).astype(jnp.int32)
    return block_e, slot_tok, dst_all, meta, slot_w.reshape(n_slots, 1), n_spare


class _Tiles(NamedTuple):
    rows: int
    sb_q: int
    sb_heads: int
    diff_q: int
    diff_k: int
    ff_splits: int
    dense_ff_splits: int


def _tiles(n, seq, branch_width):
    return _Tiles(rows=min(512, n), sb_q=min(256, seq), sb_heads=branch_width // HEAD_DIM,
                  diff_q=min(1024, seq), diff_k=min(1024, seq), ff_splits=2, dense_ff_splits=1)


def kernel(x, norm_mix_g, w_in, diff_q_norm_g, diff_k_norm_g, diff_lambda, diff_subln_g,
           w_gate, b_gate, w_branch_sb, w_branch_diff, w_out, norm_ffn_g,
           ffn_w_gate_up, ffn_w_down, moe_w_router, moe_w1, moe_w3, moe_w2):
    batch, seq, d = x.shape
    n = batch * seq
    depth = w_in.shape[0]
    t = _tiles(n, seq, w_in.shape[2] // 6)
    xf = x.reshape(n, d).astype(F32)
    bf = lambda w: w.astype(BF16)
    for i in range(depth):
        sbk, dfk, sbq, sbv, dfq, dfv = _proj_call(
            xf, norm_mix_g[i], w_in[i], diff_q_norm_g[i], diff_k_norm_g[i], tm=t.rows)
        y_sb = _sb_call(sbq, sbk, sbv, batch=batch, tq=t.sb_q, n_heads=t.sb_heads)
        lambda_init = 0.8 - 0.6 * math.exp(-0.3 * i)
        y_df = _diff_call(dfq, dfk, dfv, diff_lambda[i], diff_subln_g[i],
                          batch=batch, tq=t.diff_q, tk=t.diff_k, lambda_init=lambda_init)
        j = i // 2
        dense = i % 2 == 0
        outs = _merge_call(xf, y_sb, y_df, norm_mix_g[i], bf(w_gate[i]), b_gate[i],
                           bf(w_branch_sb[i]), bf(w_branch_diff[i]), bf(w_out[i]),
                           norm_ffn_g[i], None if dense else moe_w_router[j], tm=t.rows)
        if dense:
            xf, h2 = outs
            xf = _swiglu_call(xf, h2, bf(ffn_w_gate_up[j]), bf(ffn_w_down[j]),
                              tm=t.rows, nf=t.dense_ff_splits)
        else:
            xf, h2, logits = outs
            block_e, slot_tok, dst_all, meta, slot_w, n_spare = _route(logits[:, :N_EXPERTS])
            y = _expert_call(block_e, slot_tok, dst_all, meta, h2, slot_w,
                             bf(moe_w1[j]), bf(moe_w3[j]), bf(moe_w2[j]),
                             n_spare=n_spare, nf=t.ff_splits)
            xf = _combine_call(xf, y, tm=t.rows)
    return xf.reshape(batch, seq, d).astype(x.dtype)
```

```python
import functools
import math
from typing import NamedTuple

import jax
import jax.numpy as jnp
from jax import lax
from jax.experimental import pallas as pl
from jax.experimental.pallas import tpu as pltpu

F32 = jnp.float32
BF16 = jnp.bfloat16

HEAD_DIM = 64
NORM_EPS = 1e-6
N_EXPERTS = 8
TOP_K = 2
MOE_BLOCK = 512
LANES = 128
TOKEN_TILE_ROWS = 8
VMEM_LIMIT = 56 * 1024 * 1024

EXP_UNDERFLOW = -104.0
EXP2_UNDERFLOW = 152.0
MASKED = -1e30
LOG2E = math.log2(math.e)
V_ROWS = LANES + 16


def _dot(a, b):
    return jnp.dot(a, b, preferred_element_type=F32)


def _dot_nt(a, b):
    return lax.dot_general(a, b, (((1,), (1,)), ((), ())), preferred_element_type=F32)


def _rms(x, g):
    ms = jnp.mean(x * x, axis=-1, keepdims=True)
    return (x * lax.rsqrt(ms + NORM_EPS)) * g


def _split(x):
    hi = x.astype(BF16)
    return hi, (x - hi.astype(F32)).astype(BF16)


def _store_token_tiles(ref, x):
    rows = x.shape[0]
    for j in range(TOKEN_TILE_ROWS):
        ref[pl.ds(j, rows, stride=TOKEN_TILE_ROWS), :] = x[:, j * LANES:(j + 1) * LANES]


def _load_token_tiles(ref, rows):
    return jnp.concatenate([ref[pl.ds(j, rows, stride=TOKEN_TILE_ROWS), :]
                            for j in range(TOKEN_TILE_ROWS)], axis=1)


def _params(sem, vmem=VMEM_LIMIT):
    return pltpu.CompilerParams(dimension_semantics=sem, vmem_limit_bytes=vmem)


def _proj_kernel(x_ref, g_ref, wk_ref, wt_ref, bd_ref, gq_ref, gk_ref,
                 sbk_ref, dfk_ref, sbq_ref, sbv_ref, dfq_ref, dfv_ref, *, width):
    scale = HEAD_DIM ** -0.5
    h = _rms(x_ref[...], g_ref[...]).astype(BF16)
    bd = bd_ref[...]

    def feature_major(j):
        return _dot_nt(wt_ref[j * width:(j + 1) * width, :], h)

    sbk_ref[...] = _dot(h, wk_ref[:, :width]).astype(BF16)
    k = _dot(h, wk_ref[:, width:])
    hi, lo = _split(k * k)
    ss = _dot(hi, bd) + _dot(lo, bd)
    dfk_ref[...] = ((k * lax.rsqrt(ss * (1.0 / HEAD_DIM) + NORM_EPS)) * gk_ref[...]).astype(BF16)

    sbq_ref[...] = (feature_major(0) * scale).astype(BF16)
    sbv_ref[...] = feature_major(1).astype(BF16)
    q = feature_major(2)
    hi, lo = _split(q * q)
    ss = _dot(bd, hi) + _dot(bd, lo)
    q = (q * lax.rsqrt(ss * (1.0 / HEAD_DIM) + NORM_EPS)) * gq_ref[...]
    dfq_ref[...] = (q * (scale * LOG2E)).astype(BF16)
    v = feature_major(3).astype(BF16)
    for hd in range(width // LANES):
        dfv_ref[hd * V_ROWS:hd * V_ROWS + LANES, :] = v[hd * LANES:(hd + 1) * LANES, :]
        dfv_ref[hd * V_ROWS + LANES:(hd + 1) * V_ROWS, :] = jnp.ones(
            (V_ROWS - LANES, v.shape[1]), BF16)


def _proj_call(x, g, w_in, gq, gk, *, tm):
    n, d = x.shape
    width = w_in.shape[1] // 6
    cols = lambda j: w_in[:, j * width:(j + 1) * width]
    wk = jnp.concatenate([cols(1), cols(4)], axis=1).astype(BF16)
    wt = jnp.concatenate([cols(0), cols(2), cols(3), cols(5)], axis=1).T.astype(BF16)
    grp = lax.broadcasted_iota(jnp.int32, (width, width), 0) // HEAD_DIM
    bd = (grp == grp.T).astype(BF16)
    const = lambda a: pl.BlockSpec(a.shape, lambda i: (0, 0))
    tok_major = pl.BlockSpec((tm, width), lambda i: (i, 0))
    feat_major = pl.BlockSpec((width, tm), lambda i: (0, i))
    v_rows = width // LANES * V_ROWS
    args = [x, g.reshape(1, d), wk, wt, bd,
            jnp.tile(gq, width // HEAD_DIM).reshape(width, 1),
            jnp.tile(gk, width // HEAD_DIM).reshape(1, width)]
    return pl.pallas_call(
        functools.partial(_proj_kernel, width=width),
        grid=(n // tm,),
        in_specs=[pl.BlockSpec((tm, d), lambda i: (i, 0))] + [const(a) for a in args[1:]],
        out_specs=[tok_major] * 2 + [feat_major] * 3 + [pl.BlockSpec((v_rows, tm), lambda i: (0, i))],
        out_shape=[jax.ShapeDtypeStruct((n, width), BF16)] * 2
        + [jax.ShapeDtypeStruct((width, n), BF16)] * 3 + [jax.ShapeDtypeStruct((v_rows, n), BF16)],
        compiler_params=_params(("parallel",)),
        name="proj",
    )(*args)


def _sb_kernel(q_ref, k_ref, v_ref, o_ref, acc_ref, mask_ref, later_ref, *scratch, tq, n_heads):
    qi = pl.program_id(2)
    l_refs, row_refs, lb_refs, hi_refs, lo_refs = (
        scratch[j * n_heads:(j + 1) * n_heads] for j in range(5))

    @pl.when(qi == 0)
    def _():
        key = lax.broadcasted_iota(jnp.int32, (tq, tq), 0)
        other = lax.broadcasted_iota(jnp.int32, (tq, tq), 1)
        mask_ref[0] = jnp.where(key < other, 1.0, 0.0).astype(F32)
        mask_ref[1] = jnp.ones((tq, tq), F32)
        later_ref[...] = jnp.where(other > key, 1.0, 0.0).astype(BF16)

    feat = lax.broadcasted_iota(jnp.int32, q_ref.shape, 0)
    q = q_ref[...]
    q_heads = [jnp.where((feat >= hh * HEAD_DIM) & (feat < (hh + 1) * HEAD_DIM), q,
                         jnp.zeros_like(q)) for hh in range(n_heads)]

    acc_ref[...] = jnp.zeros_like(acc_ref)
    for hh in range(n_heads):
        l_refs[hh][...] = jnp.zeros_like(l_refs[hh])

    def logits(hh, kb):
        start = pl.multiple_of(kb * tq, tq)
        z = _dot(k_ref[pl.ds(start, tq), :], q_heads[hh])
        sp = jnp.maximum(z, 0.0) + jnp.log(1.0 + jnp.exp(-jnp.abs(z)))
        log_keep = -sp * mask_ref[jnp.where(kb == qi, 0, 1)]
        hi, lo = _split(log_keep)
        hi_refs[hh][...] = hi
        lo_refs[hh][...] = lo
        lb_refs[hh][...] = z - sp
        row_refs[hh][...] = log_keep[:1, :]

    def weights(hh, kb):
        feats = slice(hh * HEAD_DIM, (hh + 1) * HEAD_DIM)
        start = pl.multiple_of(kb * tq, tq)
        later_keys = later_ref[...]
        later = _dot(later_keys, hi_refs[hh][...]) + _dot(later_keys, lo_refs[hh][...])
        carry_l = l_refs[hh][...]
        w = jnp.exp(lb_refs[hh][...] + later + carry_l) * mask_ref[jnp.where(kb == qi, 0, 1)]
        acc_ref[feats, :] += _dot(v_ref[feats, pl.ds(start, tq)], w.astype(BF16))
        l_new = carry_l + later[:1, :] + row_refs[hh][...]
        l_refs[hh][...] = l_new
        return jnp.max(l_new)

    logits(0, qi)

    def cond(c):
        it, lm = c
        return jnp.logical_and(it <= qi, lm > EXP_UNDERFLOW)

    def body(c):
        it, _ = c
        kb = qi - it
        lm = None
        for hh in range(n_heads):
            if hh + 1 < n_heads:
                logits(hh + 1, kb)
            else:
                logits(0, jnp.maximum(kb - 1, 0))
            head_max = weights(hh, kb)
            lm = head_max if lm is None else jnp.maximum(lm, head_max)
        return it + 1, lm

    lax.while_loop(cond, body, (jnp.int32(0), jnp.float32(0.0)))
    o_ref[...] = acc_ref[...].T.astype(o_ref.dtype)


def _sb_call(q, k, v, *, batch, tq, n_heads):
    n, width = k.shape
    seq = n // batch
    nq = seq // tq
    fw = n_heads * HEAD_DIM
    return pl.pallas_call(
        functools.partial(_sb_kernel, tq=tq, n_heads=n_heads),
        grid=(batch, width // fw, nq),
        in_specs=[pl.BlockSpec((fw, tq), lambda b, h, i: (h, b * nq + i)),
                  pl.BlockSpec((seq, fw), lambda b, h, i: (b, h)),
                  pl.BlockSpec((fw, seq), lambda b, h, i: (h, b))],
        out_specs=pl.BlockSpec((tq, fw), lambda b, h, i: (b * nq + i, h)),
        out_shape=jax.ShapeDtypeStruct((n, width), BF16),
        scratch_shapes=[pltpu.VMEM((fw, tq), F32), pltpu.VMEM((2, tq, tq), F32),
                        pltpu.VMEM((tq, tq), BF16)]
        + [pltpu.VMEM((1, tq), F32)] * (2 * n_heads) + [pltpu.VMEM((tq, tq), F32)] * n_heads
        + [pltpu.VMEM((tq, tq), BF16)] * (2 * n_heads),
        compiler_params=_params(("parallel", "parallel", "arbitrary")),
        name="sb_attn",
    )(q, k, v)


def _diff_kernel(slopes_ref, lam_ref, g_ref, q_ref, k_ref, v_ref, o_ref, bias_ref,
                 s0_ref, s1_ref, mb0_ref, mb1_ref, m0_ref, m1_ref, acc0_ref, acc1_ref, kmax_ref,
                 *, tq, tk, lambda_init):
    h = pl.program_id(1)
    qi = pl.program_id(2)
    slope = slopes_ref[0, h]
    inv_slope = slopes_ref[1, h]
    nsub = tk // tq
    s_refs, mb_refs = (s0_ref, s1_ref), (mb0_ref, mb1_ref)
    m_refs, acc_refs = (m0_ref, m1_ref), (acc0_ref, acc1_ref)

    @pl.when(qi == 0)
    def _():
        key = lax.broadcasted_iota(jnp.int32, (tk, tq), 0)
        query = lax.broadcasted_iota(jnp.int32, (tk, tq), 1)
        key_ahead = key - query
        rel_bias = slope * key_ahead.astype(F32)
        bias_ref[nsub] = rel_bias
        for r in range(nsub):
            bias_ref[r] = jnp.where(key_ahead <= r * tq, rel_bias, MASKED)
        kf = k_ref[...].astype(F32)
        kmax_ref[0] = jnp.max(jnp.sum(kf * kf, axis=1, keepdims=True))

    feat = lax.broadcasted_iota(jnp.int32, (LANES, tq), 0)
    q = q_ref[...]
    q_maps = [jnp.where((feat >= c * HEAD_DIM) & (feat < (c + 1) * HEAD_DIM), q, jnp.zeros_like(q))
              for c in range(2)]

    for c in range(2):
        m_refs[c][...] = jnp.full_like(m_refs[c], MASKED)
        acc_refs[c][...] = jnp.zeros_like(acc_refs[c])

    last = (qi * tq) // tk

    qf = q.astype(F32)
    qk = jnp.max(jnp.sqrt(jnp.sum(qf * qf, axis=0, keepdims=True) * kmax_ref[0]))
    skip_dist = (EXP2_UNDERFLOW + 2.01 * qk) * inv_slope
    first = jnp.clip(((qi * tq).astype(F32) - skip_dist) * (1.0 / tk), 0.0, last.astype(F32))
    first = first.astype(jnp.int32)

    def offset(kb):
        return -slope * (qi * tq - kb * tk).astype(F32)

    half = tk // 2

    def scores(c, kb):
        sel = jnp.where(kb == last, qi % nsub, nsub)
        mb = None
        for part in range(2):
            rows = pl.ds(part * half, half)
            start = pl.multiple_of(kb * tk + part * half, half)
            s = _dot(k_ref[pl.ds(start, half), :], q_maps[c]) + bias_ref[sel, rows, :]
            s_refs[c][rows, :] = s
            part_max = jnp.max(s, axis=0, keepdims=True)
            mb = part_max if mb is None else jnp.maximum(mb, part_max)
        mb_refs[c][...] = mb + offset(kb)

    def accumulate(c, kb):
        m_old = m_refs[c][...]
        m_new = jnp.maximum(m_old, mb_refs[c][...])
        alpha = jnp.exp2(m_old - m_new)
        shift = m_new - offset(kb)
        pv = None
        for part in range(2):
            start = pl.multiple_of(kb * tk + part * half, half)
            p = jnp.exp2(s_refs[c][pl.ds(part * half, half), :] - shift)
            part_pv = _dot(v_ref[:, pl.ds(start, half)], p.astype(BF16))
            pv = part_pv if pv is None else pv + part_pv
        acc_refs[c][...] = alpha * acc_refs[c][...] + pv
        m_refs[c][...] = m_new

    scores(0, first)

    def step(kb):
        scores(1, kb)
        accumulate(0, kb)
        scores(0, jnp.minimum(kb + 1, last))
        accumulate(1, kb)

    def pair(j, carry):
        step(first + 2 * j)
        step(first + 2 * j + 1)
        return carry

    n_blocks = last + 1 - first
    lax.fori_loop(0, n_blocks // 2, pair, 0)

    @pl.when(n_blocks % 2 == 1)
    def _():
        step(last)

    lp = lam_ref[...]
    lam = (jnp.exp(jnp.sum(lp[0:1] * lp[1:2], axis=-1, keepdims=True))
           - jnp.exp(jnp.sum(lp[2:3] * lp[3:4], axis=-1, keepdims=True)) + lambda_init)
    a0, a1 = acc0_ref[...], acc1_ref[...]
    o = (a0[:LANES] * (1.0 / a0[LANES:LANES + 1])
         - lam * (a1[:LANES] * (1.0 / a1[LANES:LANES + 1])))
    ms = jnp.mean(o * o, axis=0, keepdims=True)
    o = ((o * lax.rsqrt(ms + NORM_EPS)) * g_ref[...]) * (1.0 - lambda_init)
    o_ref[...] = o.T.astype(o_ref.dtype)


def _diff_call(q, k, v, lam_params, subln_g, *, batch, tq, tk, lambda_init):
    n, width = k.shape
    heads = width // LANES
    assert tk % tq == 0
    slopes = [2.0 ** (-8.0 * (h + 1) / heads) * LOG2E for h in range(heads)]
    slopes = jnp.asarray([slopes, [1.0 / s for s in slopes]], dtype=F32)
    seq = n // batch
    nq = seq // tq
    return pl.pallas_call(
        functools.partial(_diff_kernel, tq=tq, tk=tk, lambda_init=lambda_init),
        grid=(batch, heads, nq),
        in_specs=[pl.BlockSpec(memory_space=pltpu.SMEM),
                  pl.BlockSpec(lam_params.shape, lambda b, h, i: (0, 0)),
                  pl.BlockSpec((LANES, 1), lambda b, h, i: (0, 0)),
                  pl.BlockSpec((LANES, tq), lambda b, h, i: (h, b * nq + i)),
                  pl.BlockSpec((seq, LANES), lambda b, h, i: (b, h)),
                  pl.BlockSpec((V_ROWS, seq), lambda b, h, i: (h, b))],
        out_specs=pl.BlockSpec((tq, LANES), lambda b, h, i: (b * nq + i, h)),
        out_shape=jax.ShapeDtypeStruct((n, width), BF16),
        scratch_shapes=[pltpu.VMEM((tk // tq + 1, tk, tq), F32)]
        + [pltpu.VMEM((tk, tq), F32)] * 2 + [pltpu.VMEM((1, tq), F32)] * 4
        + [pltpu.VMEM((V_ROWS, tq), F32)] * 2 + [pltpu.SMEM((1,), F32)],
        compiler_params=_params(("parallel", "parallel", "arbitrary")),
        name="diff_attn",
    )(slopes, lam_params.astype(F32), subln_g.reshape(LANES, 1).astype(F32), q, k, v)


def _merge_kernel(*refs, router):
    (x_ref, ysb_ref, ydf_ref, gmix_ref, wg_ref, bg_ref, wsb_ref, wdf_ref, wo_ref,
     gffn_ref) = refs[:10]
    rest = refs[10:]
    x = x_ref[...]
    d = x.shape[-1]
    h = _rms(x, gmix_ref[...]).astype(BF16)
    gates = jax.nn.sigmoid(_dot(h, wg_ref[...]) + bg_ref[...])
    merged = (gates[:, :d] * _dot(ysb_ref[...], wsb_ref[...])
              + gates[:, d:] * _dot(ydf_ref[...], wdf_ref[...]))
    xn = x + _dot(merged.astype(BF16), wo_ref[...])
    h2 = _rms(xn, gffn_ref[...])
    if router:
        wr_ref, xo_ref, h2_ref, lg_ref = rest
        lg_ref[...] = _dot(h2.astype(BF16), wr_ref[...])
        _store_token_tiles(h2_ref, h2)
    else:
        xo_ref, h2_ref = rest
        h2_ref[...] = h2.astype(h2_ref.dtype)
    xo_ref[...] = xn


def _merge_call(x, ysb, ydf, gmix, wg, bg, wsb, wdf, wo, gffn, w_router, *, tm):
    n, d = x.shape
    router = w_router is not None
    row = lambda w: pl.BlockSpec((tm, w), lambda i: (i, 0))
    const = lambda a: pl.BlockSpec(a.shape, lambda i: (0, 0))
    args = [x, ysb, ydf, gmix.reshape(1, d), wg, bg.reshape(1, -1), wsb, wdf, wo,
            gffn.reshape(1, d)]
    in_specs = [row(d), row(ysb.shape[1]), row(ydf.shape[1])] + [const(a) for a in args[3:]]
    out_specs = [row(d), row(d)]
    out_shape = [jax.ShapeDtypeStruct((n, d), F32), jax.ShapeDtypeStruct((n, d), BF16)]
    if router:
        assert d == TOKEN_TILE_ROWS * LANES
        out_specs[1] = pl.BlockSpec((tm * TOKEN_TILE_ROWS, LANES), lambda i: (i, 0))
        out_shape[1] = jax.ShapeDtypeStruct((n * TOKEN_TILE_ROWS, LANES), F32)
        wr = jnp.zeros((d, LANES), BF16).at[:, :w_router.shape[1]].set(w_router.astype(BF16))
        args.append(wr)
        in_specs.append(const(wr))
        out_specs.append(row(LANES))
        out_shape.append(jax.ShapeDtypeStruct((n, LANES), F32))
    return pl.pallas_call(
        functools.partial(_merge_kernel, router=router),
        grid=(n // tm,),
        in_specs=in_specs,
        out_specs=out_specs,
        out_shape=out_shape,
        compiler_params=_params(("parallel",)),
        name="merge_router" if router else "merge",
    )(*args)


def _swiglu_kernel(x_ref, h_ref, wg_ref, wu_ref, wd_ref, o_ref, acc_ref):
    f = pl.program_id(1)

    @pl.when(f == 0)
    def _():
        acc_ref[...] = jnp.zeros_like(acc_ref)

    h = h_ref[...]
    g = _dot(h, wg_ref[...])
    u = _dot(h, wu_ref[...])
    act = (g * jax.nn.sigmoid(g)) * u
    acc_ref[...] += _dot(act.astype(BF16), wd_ref[...])

    @pl.when(f == pl.num_programs(1) - 1)
    def _():
        o_ref[...] = x_ref[...] + acc_ref[...]


def _swiglu_call(x, h, w_gate_up, w_down, *, tm, nf):
    n, d = x.shape
    d_ff = w_down.shape[0]
    tf = d_ff // nf
    assert tf * nf == d_ff and tf % LANES == 0
    row = pl.BlockSpec((tm, d), lambda i, f: (i, 0))
    mode = dict(pipeline_mode=pl.Buffered(1)) if nf == 1 else {}
    return pl.pallas_call(
        _swiglu_kernel,
        grid=(n // tm, nf),
        in_specs=[row, row,
                  pl.BlockSpec((d, tf), lambda i, f: (0, f), **mode),
                  pl.BlockSpec((d, tf), lambda i, f: (0, f + nf), **mode),
                  pl.BlockSpec((tf, d), lambda i, f: (f, 0), **mode)],
        out_specs=row,
        out_shape=jax.ShapeDtypeStruct((n, d), F32),
        scratch_shapes=[pltpu.VMEM((tm, d), F32)],
        compiler_params=_params(("parallel", "arbitrary")),
        name="swiglu",
    )(x, h, w_gate_up, w_gate_up, w_down)


def _expert_kernel(be_ref, tok_ref, dst_ref, meta_ref, h_hbm, sw_ref, w1_ref, w3_ref, w2_ref,
                   y_hbm, xbuf, acc_ref, ybuf, zrow_ref, gsem, ssem, *, n_spare):
    del be_ref
    i = pl.program_id(0)
    f = pl.program_id(1)
    nf = pl.num_programs(1)
    nused = meta_ref[0]
    rows = acc_ref.shape[0]

    def tile_of(row):
        return pl.ds(pl.multiple_of(row * TOKEN_TILE_ROWS, TOKEN_TILE_ROWS), TOKEN_TILE_ROWS)

    def gather_copy(blk, r):
        return pltpu.make_async_copy(h_hbm.at[tile_of(tok_ref[blk * rows + r])],
                                     xbuf.at[blk % 2, tile_of(r)], gsem.at[blk % 2])

    def scatter_copy(blk, r):
        return pltpu.make_async_copy(ybuf.at[(blk + 2) % 2, tile_of(r)],
                                     y_hbm.at[tile_of(dst_ref[(blk + 1) * rows + r])],
                                     ssem.at[(blk + 2) % 2])

    def for_rows(lo, n, fn):
        def body(r, carry):
            fn(lo + r)
            return carry
        lax.fori_loop(0, n, body, 0, unroll=8)

    @pl.when(jnp.logical_and(i == 0, f == 0))
    def _():
        ybuf[1] = jnp.zeros(ybuf.shape[1:], ybuf.dtype)
        for_rows(0, rows, lambda r: gather_copy(0, r).start())

    @pl.when(jnp.logical_and(f == 0, i <= nused))
    def _():
        for_rows(0, rows, lambda r: gather_copy(i, r).wait())

    @pl.when(jnp.logical_and(f == 0, i < nused))
    def _():
        acc_ref[...] = jnp.zeros_like(acc_ref)

    chunk = rows // nf

    @pl.when(i < nused)
    def _():
        for r in range(chunk):
            gather_copy(i + 1, f * chunk + r).start(priority=r % 2)
            scatter_copy(i - 1, f * chunk + r).start(priority=r % 2)
        xb = _load_token_tiles(xbuf.at[i % 2], rows).astype(BF16)
        g = _dot(xb, w1_ref[...])
        u = _dot(xb, w3_ref[...])
        hid = (g * jax.nn.sigmoid(g)) * u
        acc_ref[...] += _dot(hid.astype(BF16), w2_ref[...])

    @pl.when(i == nused)
    def _():
        for_rows(f * chunk, chunk, lambda r: scatter_copy(i - 1, r).start())

    @pl.when(f == nf - 1)
    def _():
        @pl.when(jnp.logical_and(i < nused, i >= 1))
        def _():
            for_rows(0, rows, lambda r: scatter_copy(i - 2, r).wait())

        @pl.when(i < nused)
        def _():
            _store_token_tiles(ybuf.at[i % 2], acc_ref[...] * sw_ref[...])

        @pl.when(i == nused)
        def _():
            @pl.when(i >= 1)
            def _():
                for_rows(0, rows, lambda r: scatter_copy(i - 2, r).wait())
            for_rows(0, rows, lambda r: scatter_copy(i - 1, r).wait())

            zrow_ref[...] = jnp.zeros_like(zrow_ref)
            n_pad = meta_ref[1]
            spare_lo = y_hbm.shape[0] // TOKEN_TILE_ROWS - rows - n_spare

            def zero_copy(r):
                return pltpu.make_async_copy(zrow_ref, y_hbm.at[tile_of(spare_lo + r)],
                                             ssem.at[0])

            def fill(r, carry):
                zero_copy(r).start()
                return carry

            def drain(r, carry):
                zero_copy(r).wait()
                return carry

            lax.fori_loop(n_pad, n_spare, fill, 0)
            lax.fori_loop(n_pad, n_spare, drain, 0)


def _expert_call(block_e, slot_tok, dst_all, meta, h, slot_w, w1, w3, w2, *, n_spare, nf):
    n_slots = slot_tok.shape[0]
    n_tok = h.shape[0] // TOKEN_TILE_ROWS
    n_out = n_tok * TOP_K + n_spare + MOE_BLOCK
    n_blocks = n_slots // MOE_BLOCK
    d = w1.shape[1]
    d_ff = w1.shape[2]
    tf = d_ff // nf
    tile_rows = MOE_BLOCK * TOKEN_TILE_ROWS
    assert tf * nf == d_ff and tf % LANES == 0 and MOE_BLOCK % nf == 0
    assert d == TOKEN_TILE_ROWS * LANES and h.shape[1] == LANES
    def tile(i, f, meta):
        return jnp.where(i < meta[0], f, nf - 1)

    grid_spec = pltpu.PrefetchScalarGridSpec(
        num_scalar_prefetch=4,
        grid=(n_blocks, nf),
        in_specs=[pl.BlockSpec(memory_space=pl.ANY),
                  pl.BlockSpec((MOE_BLOCK, 1), lambda i, f, be, *_: (i, 0)),
                  pl.BlockSpec((None, d, tf), lambda i, f, be, tok, dst, meta:
                               (be[i], 0, tile(i, f, meta))),
                  pl.BlockSpec((None, d, tf), lambda i, f, be, tok, dst, meta:
                               (be[i], 0, tile(i, f, meta))),
                  pl.BlockSpec((None, tf, d), lambda i, f, be, tok, dst, meta:
                               (be[i], tile(i, f, meta), 0))],
        out_specs=pl.BlockSpec(memory_space=pl.ANY),
        scratch_shapes=[pltpu.VMEM((2, tile_rows, LANES), F32),
                        pltpu.VMEM((MOE_BLOCK, d), F32),
                        pltpu.VMEM((2, tile_rows, LANES), F32),
                        pltpu.VMEM((TOKEN_TILE_ROWS, LANES), F32),
                        pltpu.SemaphoreType.DMA((2,)),
                        pltpu.SemaphoreType.DMA((2,))],
    )
    return pl.pallas_call(
        functools.partial(_expert_kernel, n_spare=n_spare),
        grid_spec=grid_spec,
        out_shape=jax.ShapeDtypeStruct((n_out * TOKEN_TILE_ROWS, LANES), F32),
        compiler_params=_params(("arbitrary", "arbitrary")),
        name="moe_experts",
    )(block_e, slot_tok, dst_all, meta, h, slot_w, w1, w3, w2)


def _combine_kernel(x_ref, y0_ref, y1_ref, o_ref):
    tm = x_ref.shape[0]
    o_ref[...] = x_ref[...] + (_load_token_tiles(y0_ref, tm) + _load_token_tiles(y1_ref, tm))


def _combine_call(x, y, *, tm):
    n, d = x.shape
    steps = n // tm
    y_rows = tm * TOKEN_TILE_ROWS
    return pl.pallas_call(
        _combine_kernel,
        grid=(steps,),
        in_specs=[pl.BlockSpec((tm, d), lambda i: (i, 0)),
                  pl.BlockSpec((y_rows, LANES), lambda i: (i, 0)),
                  pl.BlockSpec((y_rows, LANES), lambda i: (i + steps, 0))],
        out_specs=pl.BlockSpec((tm, d), lambda i: (i, 0)),
        out_shape=jax.ShapeDtypeStruct((n, d), F32),
        compiler_params=_params(("parallel",)),
        name="moe_combine",
    )(x, y, y)


def _route(logits):
    n_tok = logits.shape[0]
    n_assign = n_tok * TOP_K
    n_slots = -(-n_assign // MOE_BLOCK) * MOE_BLOCK + N_EXPERTS * MOE_BLOCK
    n_blocks = n_slots // MOE_BLOCK
    top_logit, top_idx = lax.top_k(logits, TOP_K)
    top_w = jax.nn.softmax(top_logit, axis=-1)
    onehot = jnp.sum(jax.nn.one_hot(top_idx, N_EXPERTS, dtype=jnp.int32), axis=1)
    csum = jnp.cumsum(onehot, axis=0)
    counts = csum[-1]
    padded = (counts + MOE_BLOCK - 1) // MOE_BLOCK * MOE_BLOCK
    padded_end = jnp.cumsum(padded)
    padded_start = padded_end - padded
    rank = jnp.take_along_axis(csum - onehot, top_idx, axis=1)
    slots = (padded_start[top_idx] + rank).astype(jnp.int32)
    dst = (jnp.arange(TOP_K, dtype=jnp.int32)[None, :] * n_tok
           + jnp.arange(n_tok, dtype=jnp.int32)[:, None])
    slot_dst = jnp.full((n_slots,), -1, jnp.int32).at[slots.reshape(-1)].set(dst.reshape(-1))
    real = slot_dst >= 0
    n_spare = n_slots - n_assign
    pad_rank = jnp.cumsum(jnp.logical_not(real).astype(jnp.int32)) - 1
    slot_tok = jnp.where(real, slot_dst % n_tok, 0)
    w_kmajor = top_w.T.reshape(-1)
    slot_w = jnp.where(real, w_kmajor[jnp.maximum(slot_dst, 0)], 0.0)
    slot_dst = jnp.where(real, slot_dst, n_assign + pad_rank)
    virtual = n_assign + n_spare + jnp.arange(MOE_BLOCK, dtype=jnp.int32)
    dst_all = jnp.concatenate([virtual, slot_dst])
    block_e = jnp.minimum(
        jnp.searchsorted(padded_end, jnp.arange(n_blocks) * MOE_BLOCK, side='right'),
        N_EXPERTS - 1).astype(jnp.int32)
    meta = jnp.stack([padded_end[-1] // MOE_BLOCK, padded_end[-1] - n_assign]).astype(jnp.int32)
    return block_e, slot_tok, dst_all, meta, slot_w.reshape(n_slots, 1), n_spare


class _Tiles(NamedTuple):
    rows: int
    sb_q: int
    sb_heads: int
    diff_q: int
    diff_k: int
    ff_splits: int
    dense_ff_splits: int


def _tiles(n, seq, branch_width):
    return _Tiles(rows=min(512, n), sb_q=min(256, seq), sb_heads=branch_width // HEAD_DIM,
                  diff_q=min(1024, seq), diff_k=min(1024, seq), ff_splits=2, dense_ff_splits=1)


def kernel(x, norm_mix_g, w_in, diff_q_norm_g, diff_k_norm_g, diff_lambda, diff_subln_g,
           w_gate, b_gate, w_branch_sb, w_branch_diff, w_out, norm_ffn_g,
           ffn_w_gate_up, ffn_w_down, moe_w_router, moe_w1, moe_w3, moe_w2):
    batch, seq, d = x.shape
    n = batch * seq
    depth = w_in.shape[0]
    t = _tiles(n, seq, w_in.shape[2] // 6)
    xf = x.reshape(n, d).astype(F32)
    bf = lambda w: w.astype(BF16)
    for i in range(depth):
        sbk, dfk, sbq, sbv, dfq, dfv = _proj_call(
            xf, norm_mix_g[i], w_in[i], diff_q_norm_g[i], diff_k_norm_g[i], tm=t.rows)
        y_sb = _sb_call(sbq, sbk, sbv, batch=batch, tq=t.sb_q, n_heads=t.sb_heads)
        lambda_init = 0.8 - 0.6 * math.exp(-0.3 * i)
        y_df = _diff_call(dfq, dfk, dfv, diff_lambda[i], diff_subln_g[i],
                          batch=batch, tq=t.diff_q, tk=t.diff_k, lambda_init=lambda_init)
        j = i // 2
        dense = i % 2 == 0
        outs = _merge_call(xf, y_sb, y_df, norm_mix_g[i], bf(w_gate[i]), b_gate[i],
                           bf(w_branch_sb[i]), bf(w_branch_diff[i]), bf(w_out[i]),
                           norm_ffn_g[i], None if dense else moe_w_router[j], tm=t.rows)
        if dense:
            xf, h2 = outs
            xf = _swiglu_call(xf, h2, bf(ffn_w_gate_up[j]), bf(ffn_w_down[j]),
                              tm=t.rows, nf=t.dense_ff_splits)
        else:
            xf, h2, logits = outs
            block_e, slot_tok, dst_all, meta, slot_w, n_spare = _route(logits[:, :N_EXPERTS])
            y = _expert_call(block_e, slot_tok, dst_all, meta, h2, slot_w,
                             bf(moe_w1[j]), bf(moe_w3[j]), bf(moe_w2[j]),
                             n_spare=n_spare, nf=t.ff_splits)
            xf = _combine_call(xf, y, tm=t.rows)
    return xf.reshape(batch, seq, d).astype(x.dtype)
```
